```python
import jax, jax.numpy as jnp
from jax import lax
import numpy as np

D_MODEL = 1024
BATCH = 16
SEQ = 2048
DEPTH = 4
DEC_BATCH = 16
DEC_SEQ = 32
PAST_LEN = 2048

CHUNK = 64
Q_BLOCK = 128
PE_DIM = 256
N_MIXERS = 3
N_RET = (DEPTH + 2) // 3
N_MLA = (DEPTH + 1) // 3
N_FOX = DEPTH // 3
EPS = 1e-6
ROPE_THETA = 10000.0
RET_HEADS = 4
RET_DK = 256
RET_DV = 512
RET_QK = RET_HEADS * RET_DK
RET_VW = RET_HEADS * RET_DV
MLA_HEADS = 16
MLA_NOPE = 128
MLA_ROPE = 64
MLA_V = 128
MLA_Q_LORA = 512
MLA_KV_LORA = 256
FOX_HEADS = 16
FOX_DH = 64
FOX_W = FOX_HEADS * FOX_DH
D_FF = ((8 * D_MODEL + 3 * 256 - 1) // (3 * 256)) * 256

kernel_name = "hybrid_retention_mla_fox_streaming_step"


def rms_norm(x, g):
    xf = x.astype(jnp.float32)
    y = xf * lax.rsqrt(jnp.mean(xf * xf, axis=-1, keepdims=True) + EPS)
    return (y * g.astype(jnp.float32)).astype(x.dtype)


def rope(x, pos):
    half = x.shape[-1] // 2
    inv = ROPE_THETA ** (-jnp.arange(half, dtype=jnp.float32) / half)
    ang = pos.astype(jnp.float32)[:, None] * inv[None, :]
    cos = jnp.cos(ang)[None, :, None, :]
    sin = jnp.sin(ang)[None, :, None, :]
    xf = x.astype(jnp.float32)
    x1, x2 = xf[..., :half], xf[..., half:]
    return jnp.concatenate([x1 * cos - x2 * sin, x2 * cos + x1 * sin], axis=-1).astype(x.dtype)


def block_attention(q, k, v, chunk_causal, q_bias=None, k_bias=None):
    b, tq, h, dk = q.shape
    tk = k.shape[1]
    off = tk - tq
    blk = min(Q_BLOCK, tq)
    scale = dk ** -0.5
    outs = []
    for q0 in range(0, tq, blk):
        last = off + q0 + blk
        kend = min(tk, -(-last // CHUNK) * CHUNK) if chunk_causal else last
        qp = off + q0 + jnp.arange(blk)
        kp = jnp.arange(kend)
        s = jnp.einsum('bqhd,bkhd->bhqk', q[:, q0:q0 + blk], k[:, :kend]).astype(jnp.float32) * scale
        if q_bias is not None:
            s = s + q_bias[:, :, q0:q0 + blk, None] - k_bias[:, :, None, :kend]
        if chunk_causal:
            allowed = (kp[None, :] // CHUNK) <= (qp[:, None] // CHUNK)
        else:
            allowed = kp[None, :] <= qp[:, None]
        s = jnp.where(allowed[None, None], s, -jnp.inf)
        pr = jax.nn.softmax(s, axis=-1).astype(v.dtype)
        outs.append(jnp.einsum('bhqk,bkhd->bqhd', pr, v[:, :kend]))
    return jnp.concatenate(outs, axis=1)


def retention_scan(q, k, v, s0):
    b, t, h, dk = q.shape
    dv = v.shape[-1]
    L = min(CHUNK, t)
    n = t // L
    lg = jnp.log1p(-jnp.exp2(-5.0 - jnp.arange(h, dtype=jnp.float32)))
    idx = jnp.arange(L, dtype=jnp.float32)
    diff = idx[:, None] - idx[None, :]
    dmask = jnp.where(diff >= 0, jnp.exp(lg[:, None, None] * jnp.maximum(diff, 0.0)), 0.0)
    q_dec = jnp.exp(lg[:, None] * (idx + 1.0))[:, :, None]
    k_dec = jnp.exp(lg[:, None] * (L - 1.0 - idx))[:, :, None]
    c_dec = jnp.exp(lg * L)[:, None, None]

    def to_chunks(a):
        return a.astype(jnp.float32).reshape(b, n, L, h, a.shape[-1]).transpose(1, 0, 3, 2, 4)

    def step(s, inp):
        qc, kc, vc = inp
        att = jnp.einsum('bhld,bhmd->bhlm', qc, kc) * dmask
        o = jnp.einsum('bhlm,bhmv->bhlv', att, vc) + jnp.einsum('bhld,bhdv->bhlv', qc * q_dec, s)
        s = s * c_dec + jnp.einsum('bhmd,bhmv->bhdv', kc * k_dec, vc)
        return s, o

    s, o = lax.scan(step, s0.astype(jnp.float32), (to_chunks(q), to_chunks(k), to_chunks(v)))
    o = o.transpose(1, 0, 3, 2, 4).reshape(b, t, h, dv)
    return o, s


def head_group_norm(o, g):
    b, t, h, d = o.shape
    mu = jnp.mean(o, axis=-1, keepdims=True)
    var = jnp.mean(jnp.square(o - mu), axis=-1, keepdims=True)
    y = (o - mu) * lax.rsqrt(var + EPS)
    return y.reshape(b, t, h * d) * g.astype(jnp.float32)


def retention_mixer(xn, pos, s0, w_in, gn, w_out):
    b, t, _ = xn.shape
    proj = xn @ w_in
    q, k, v, g = jnp.split(proj, [RET_QK, 2 * RET_QK, 2 * RET_QK + RET_VW], axis=-1)
    q = rope(q.reshape(b, t, RET_HEADS, RET_DK), pos)
    k = rope(k.reshape(b, t, RET_HEADS, RET_DK), pos) * (RET_DK ** -0.5)
    v = v.reshape(b, t, RET_HEADS, RET_DV)
    o, s = retention_scan(q, k, v, s0)
    y = head_group_norm(o, gn).astype(xn.dtype)
    return (jax.nn.silu(g) * y) @ w_out, s.astype(xn.dtype)


def mla_mixer(xn, pos, lat_past, kr_past, w_in, q_norm, kv_norm, w_qb, w_kvb, gq_nope, gq_rope, gk_nope, gk_rope, w_out):
    b, t, _ = xn.shape
    proj = xn @ w_in
    cq, ckv, kr = jnp.split(proj, [MLA_Q_LORA, MLA_Q_LORA + MLA_KV_LORA], axis=-1)
    q = (rms_norm(cq, q_norm) @ w_qb).reshape(b, t, MLA_HEADS, MLA_NOPE + MLA_ROPE)
    q_nope = rms_norm(q[..., :MLA_NOPE], gq_nope)
    q_rope = rope(rms_norm(q[..., MLA_NOPE:], gq_rope), pos)
    lat = rms_norm(ckv, kv_norm)
    kr = rope(rms_norm(kr, gk_rope)[:, :, None, :], pos)[:, :, 0, :]
    lat_all = jnp.concatenate([lat_past, lat], axis=1)
    kr_all = jnp.concatenate([kr_past, kr], axis=1)
    tk = lat_all.shape[1]
    kv = (lat_all @ w_kvb).reshape(b, tk, MLA_HEADS, MLA_NOPE + MLA_V)
    k_nope = rms_norm(kv[..., :MLA_NOPE], gk_nope)
    v = kv[..., MLA_NOPE:]
    qf = jnp.concatenate([q_nope, q_rope], axis=-1)
    kf = jnp.concatenate([k_nope, jnp.broadcast_to(kr_all[:, :, None, :], (b, tk, MLA_HEADS, MLA_ROPE))], axis=-1)
    o = block_attention(qf, kf, v, chunk_causal=True)
    return o.reshape(b, t, MLA_HEADS * MLA_V) @ w_out, lat, kr


def fox_mixer(xn, k_past, v_past, lf_past, w_in, b_f, gq, gk, w_out):
    b, t, _ = xn.shape
    proj = xn @ w_in
    q, k, v, g, fl = jnp.split(proj, [FOX_W, 2 * FOX_W, 3 * FOX_W, 4 * FOX_W], axis=-1)
    q = rms_norm(q.reshape(b, t, FOX_HEADS, FOX_DH), gq)
    k = rms_norm(k.reshape(b, t, FOX_HEADS, FOX_DH), gk)
    v = v.reshape(b, t, FOX_HEADS, FOX_DH)
    lf = jax.nn.log_sigmoid((fl + b_f).astype(jnp.float32))
    k_all = jnp.concatenate([k_past, k], axis=1)
    v_all = jnp.concatenate([v_past, v], axis=1)
    lf_all = jnp.concatenate([lf_past.astype(jnp.float32), lf], axis=1)
    F = jnp.cumsum(lf_all, axis=1).transpose(0, 2, 1)
    o = block_attention(q, k_all, v_all, chunk_causal=False, q_bias=F[:, :, F.shape[2] - t:], k_bias=F)
    return (jax.nn.sigmoid(g) * o.reshape(b, t, FOX_W)) @ w_out, k, v, lf.astype(xn.dtype)


def swiglu(xn, wg, wu, wd):
    return (jax.nn.silu(xn @ wg) * (xn @ wu)) @ wd


def trunk(x, p, past_len, ret_states, mla_lat_past, mla_kr_past, fox_k_past, fox_v_past, fox_lf_past, w):
    t = x.shape[1]
    pos = past_len + jnp.arange(t)
    h = x
    new_ret, new_lat, new_kr, new_fk, new_fv, new_flf = [], [], [], [], [], []
    for i in range(DEPTH):
        kind, j = i % N_MIXERS, i // N_MIXERS
        xn = rms_norm(h, w['norm_mix'][i])
        if kind == 0:
            y, s = retention_mixer(xn, pos, ret_states[j], w['ret_w_in'][j], w['ret_gn'][j], w['ret_w_out'][j])
            new_ret.append(s)
        elif kind == 1:
            y, lat, kr = mla_mixer(xn, pos, mla_lat_past[j], mla_kr_past[j], w['mla_w_in'][j], w['mla_q_norm'][j],
                                   w['mla_kv_norm'][j], w['mla_w_qb'][j], w['mla_w_kvb'][j], w['mla_gq_nope'][j],
                                   w['mla_gq_rope'][j], w['mla_gk_nope'][j], w['mla_gk_rope'][j], w['mla_w_out'][j])
            new_lat.append(lat)
            new_kr.append(kr)
        else:
            y, fk, fv, flf = fox_mixer(xn, fox_k_past[j], fox_v_past[j], fox_lf_past[j], w['fox_w_in'][j],
                                       w['fox_b_f'][j], w['fox_gq'][j], w['fox_gk'][j], w['fox_w_out'][j])
            new_fk.append(fk)
            new_fv.append(fv)
            new_flf.append(flf)
        h = h + y.astype(h.dtype)
        h = h + swiglu(rms_norm(h, w['norm_ffn'][i]), w['ffn_w_gate'][i], w['ffn_w_up'][i], w['ffn_w_down'][i])
        gate = jax.nn.sigmoid(rms_norm(h, w['norm_pe'][i]) @ w['pe_w_gate'][i])
        h = h + gate * (p[i] @ w['pe_w_proj'][i])
    return (rms_norm(h, w['norm_final']), jnp.stack(new_ret), jnp.stack(new_lat), jnp.stack(new_kr),
            jnp.stack(new_fk), jnp.stack(new_fv), jnp.stack(new_flf))


def setup_inputs(seed: int = 0) -> dict:
    key = jax.random.key(seed)
    ks = iter(jax.random.split(key, 64))

    def nrm(shape, scale=1.0):
        return scale * jax.random.normal(next(ks), shape, jnp.float32)

    def gain(shape):
        return 1.0 + nrm(shape, 0.05)

    def lin(shape):
        return nrm(shape, shape[-2] ** -0.5)

    return {
        'x_prompt': nrm((BATCH, SEQ, D_MODEL)),
        'x_sample': nrm((DEC_BATCH, DEC_SEQ, D_MODEL)),
        'state_ret': nrm((N_RET, DEC_BATCH, RET_HEADS, RET_DK, RET_DV), 0.5),
        'cache_mla_latent': nrm((N_MLA, DEC_BATCH, PAST_LEN, MLA_KV_LORA)),
        'cache_mla_krope': nrm((N_MLA, DEC_BATCH, PAST_LEN, MLA_ROPE)),
        'cache_fox_k': nrm((N_FOX, DEC_BATCH, PAST_LEN, FOX_HEADS, FOX_DH)),
        'cache_fox_v': nrm((N_FOX, DEC_BATCH, PAST_LEN, FOX_HEADS, FOX_DH)),
        'cache_fox_logf': jax.nn.log_sigmoid(nrm((N_FOX, DEC_BATCH, PAST_LEN, FOX_HEADS)) + 3.0),
        'p_prompt': nrm((DEPTH, BATCH, SEQ, PE_DIM)),
        'p_sample': nrm((DEPTH, DEC_BATCH, DEC_SEQ, PE_DIM)),
        'norm_mix': gain((DEPTH, D_MODEL)),
        'norm_ffn': gain((DEPTH, D_MODEL)),
        'norm_pe': gain((DEPTH, D_MODEL)),
        'norm_final': gain((D_MODEL,)),
        'ret_w_in': lin((N_RET, D_MODEL, 2 * RET_QK + 2 * RET_VW)),
        'ret_gn': gain((N_RET, RET_VW)),
        'ret_w_out': lin((N_RET, RET_VW, D_MODEL)),
        'mla_w_in': lin((N_MLA, D_MODEL, MLA_Q_LORA + MLA_KV_LORA + MLA_ROPE)),
        'mla_q_norm': gain((N_MLA, MLA_Q_LORA)),
        'mla_kv_norm': gain((N_MLA, MLA_KV_LORA)),
        'mla_w_qb': lin((N_MLA, MLA_Q_LORA, MLA_HEADS * (MLA_NOPE + MLA_ROPE))),
        'mla_w_kvb': lin((N_MLA, MLA_KV_LORA, MLA_HEADS * (MLA_NOPE + MLA_V))),
        'mla_gq_nope': gain((N_MLA, MLA_NOPE)),
        'mla_gq_rope': gain((N_MLA, MLA_ROPE)),
        'mla_gk_nope': gain((N_MLA, MLA_NOPE)),
        'mla_gk_rope': gain((N_MLA, MLA_ROPE)),
        'mla_w_out': lin((N_MLA, MLA_HEADS * MLA_V, D_MODEL)),
        'fox_w_in': lin((N_FOX, D_MODEL, 4 * FOX_W + FOX_HEADS)),
        'fox_b_f': jnp.linspace(1.0, 6.0, FOX_HEADS, dtype=jnp.float32)[None, :] + nrm((N_FOX, FOX_HEADS), 0.1),
        'fox_gq': gain((N_FOX, FOX_DH)),
        'fox_gk': gain((N_FOX, FOX_DH)),
        'fox_w_out': lin((N_FOX, FOX_W, D_MODEL)),
        'ffn_w_gate': lin((DEPTH, D_MODEL, D_FF)),
        'ffn_w_up': lin((DEPTH, D_MODEL, D_FF)),
        'ffn_w_down': lin((DEPTH, D_FF, D_MODEL)),
        'pe_w_proj': lin((DEPTH, PE_DIM, D_MODEL)),
        'pe_w_gate': lin((DEPTH, D_MODEL, D_MODEL)),
    }


def reference(x_prompt, x_sample, state_ret, cache_mla_latent, cache_mla_krope, cache_fox_k, cache_fox_v,
              cache_fox_logf, p_prompt, p_sample, norm_mix, norm_ffn, norm_pe, norm_final, ret_w_in, ret_gn,
              ret_w_out, mla_w_in, mla_q_norm, mla_kv_norm, mla_w_qb, mla_w_kvb, mla_gq_nope, mla_gq_rope,
              mla_gk_nope, mla_gk_rope, mla_w_out, fox_w_in, fox_b_f, fox_gq, fox_gk, fox_w_out, ffn_w_gate,
              ffn_w_up, ffn_w_down, pe_w_proj, pe_w_gate):
    w = dict(norm_mix=norm_mix, norm_ffn=norm_ffn, norm_pe=norm_pe, norm_final=norm_final,
             ret_w_in=ret_w_in, ret_gn=ret_gn, ret_w_out=ret_w_out,
             mla_w_in=mla_w_in, mla_q_norm=mla_q_norm, mla_kv_norm=mla_kv_norm, mla_w_qb=mla_w_qb,
             mla_w_kvb=mla_w_kvb, mla_gq_nope=mla_gq_nope, mla_gq_rope=mla_gq_rope, mla_gk_nope=mla_gk_nope,
             mla_gk_rope=mla_gk_rope, mla_w_out=mla_w_out,
             fox_w_in=fox_w_in, fox_b_f=fox_b_f, fox_gq=fox_gq, fox_gk=fox_gk, fox_w_out=fox_w_out,
             ffn_w_gate=ffn_w_gate, ffn_w_up=ffn_w_up, ffn_w_down=ffn_w_down,
             pe_w_proj=pe_w_proj, pe_w_gate=pe_w_gate)
    bp = x_prompt.shape[0]
    dt = x_prompt.dtype
    y_prompt, ret_p, lat_p, kr_p, fk_p, fv_p, flf_p = trunk(
        x_prompt, p_prompt, 0,
        jnp.zeros((N_RET, bp, RET_HEADS, RET_DK, RET_DV), dt),
        jnp.zeros((N_MLA, bp, 0, MLA_KV_LORA), dt), jnp.zeros((N_MLA, bp, 0, MLA_ROPE), dt),
        jnp.zeros((N_FOX, bp, 0, FOX_HEADS, FOX_DH), dt), jnp.zeros((N_FOX, bp, 0, FOX_HEADS, FOX_DH), dt),
        jnp.zeros((N_FOX, bp, 0, FOX_HEADS), dt), w)
    y_sample, ret_s, lat_s, kr_s, fk_s, fv_s, flf_s = trunk(
        x_sample, p_sample, cache_mla_latent.shape[2], state_ret, cache_mla_latent, cache_mla_krope,
        cache_fox_k, cache_fox_v, cache_fox_logf, w)
    return (y_prompt, y_sample, ret_p, ret_s, lat_p, kr_p, lat_s, kr_s, fk_p, fv_p, flf_p, fk_s, fv_s, flf_s)
```

```python
import functools
import math

import jax
import jax.numpy as jnp
from jax import lax
from jax.experimental import pallas as pl
from jax.experimental.pallas import tpu as pltpu

D_MODEL = 1024
DEPTH = 4
CHUNK = 64
PE_DIM = 256
N_MIXERS = 3
EPS = 1e-6
ROPE_THETA = 10000.0
RET_HEADS = 4
RET_DK = 256
RET_DV = 512
RET_QK = RET_HEADS * RET_DK
RET_VW = RET_HEADS * RET_DV
MLA_HEADS = 16
MLA_NOPE = 128
MLA_ROPE = 64
MLA_V = 128
MLA_Q_LORA = 512
MLA_KV_LORA = 256
FOX_HEADS = 16
FOX_DH = 64
FOX_W = FOX_HEADS * FOX_DH
D_FF = ((8 * D_MODEL + 3 * 256 - 1) // (3 * 256)) * 256

LANES = 128
MLA_DK_PAD = 256
FOX_DK_PAD = 128
MASK_VALUE = -1e30
VMEM_LIMIT = 48 * 1024 * 1024

F32 = jnp.float32
BF16 = jnp.bfloat16


def _cparams(*sem):
    return pltpu.CompilerParams(dimension_semantics=sem, vmem_limit_bytes=VMEM_LIMIT)


def _pick(n, prefs):
    for p in prefs:
        if n % p == 0:
            return p
    return n


def _norm_mm_kernel(x_ref, g_ref, w_ref, o_ref, xn_ref):
    @pl.when(pl.program_id(1) == 0)
    def _():
        x = x_ref[...].astype(F32)
        ms = jnp.mean(x * x, axis=-1, keepdims=True)
        xn_ref[...] = (x * lax.rsqrt(ms + EPS) * g_ref[...]).astype(BF16)

    o_ref[...] = jnp.dot(xn_ref[...], w_ref[...], preferred_element_type=F32).astype(o_ref.dtype)


def norm_matmul(x, g, w, out_dtype=F32, name="norm_mm"):
    m, k = x.shape
    n = w.shape[1]
    tm = _pick(m, (1024, 512, 256))
    tn = _pick(n, (1024, 768, 512, 1408, 896, 128))
    return pl.pallas_call(
        _norm_mm_kernel,
        grid=(m // tm, n // tn),
        in_specs=[
            pl.BlockSpec((tm, k), lambda i, j: (i, 0)),
            pl.BlockSpec((1, k), lambda i, j: (0, 0)),
            pl.BlockSpec((k, tn), lambda i, j: (0, j)),
        ],
        out_specs=pl.BlockSpec((tm, tn), lambda i, j: (i, j)),
        out_shape=jax.ShapeDtypeStruct((m, n), out_dtype),
        scratch_shapes=[pltpu.VMEM((tm, k), BF16)],
        compiler_params=_cparams("parallel", "arbitrary"),
        name=name,
    )(x, g.reshape(1, k).astype(F32), w)


def _mm_kernel(a_ref, w_ref, *rest, has_res):
    o_ref = rest[-1]
    acc = jnp.dot(a_ref[...].astype(BF16), w_ref[...], preferred_element_type=F32)
    if has_res:
        acc = acc + rest[0][...]
    o_ref[...] = acc.astype(o_ref.dtype)


def matmul(a, w, res=None, out_dtype=F32, name="mm"):
    m, k = a.shape
    n = w.shape[1]
    tm = _pick(m, (1024, 512, 256, 520))
    tn = _pick(n, (1024, 512, 128))
    in_specs = [
        pl.BlockSpec((tm, k), lambda i, j: (i, 0)),
        pl.BlockSpec((k, tn), lambda i, j: (0, j)),
    ]
    args = [a, w]
    if res is not None:
        in_specs.append(pl.BlockSpec((tm, tn), lambda i, j: (i, j)))
        args.append(res)
    return pl.pallas_call(
        functools.partial(_mm_kernel, has_res=res is not None),
        grid=(m // tm, n // tn),
        in_specs=in_specs,
        out_specs=pl.BlockSpec((tm, tn), lambda i, j: (i, j)),
        out_shape=jax.ShapeDtypeStruct((m, n), out_dtype),
        compiler_params=_cparams("parallel", "arbitrary"),
        name=name,
    )(*args)


def _ffn_kernel(x_ref, g_ref, wg_ref, wu_ref, wd_ref, o_ref, xn_ref, acc_ref):
    f = pl.program_id(1)

    @pl.when(f == 0)
    def _():
        x = x_ref[...]
        ms = jnp.mean(x * x, axis=-1, keepdims=True)
        xn_ref[...] = (x * lax.rsqrt(ms + EPS) * g_ref[...]).astype(BF16)
        acc_ref[...] = x

    xn = xn_ref[...]
    gate = jnp.dot(xn, wg_ref[...], preferred_element_type=F32)
    up = jnp.dot(xn, wu_ref[...], preferred_element_type=F32)
    act = (gate * jax.nn.sigmoid(gate) * up).astype(BF16)
    acc_ref[...] += jnp.dot(act, wd_ref[...], preferred_element_type=F32)

    @pl.when(f == pl.num_programs(1) - 1)
    def _():
        o_ref[...] = acc_ref[...]


def ffn(h, g, wg, wu, wd):
    m, d = h.shape
    dff = wg.shape[1]
    tm = _pick(m, (1024, 512))
    tf = 256
    return pl.pallas_call(
        _ffn_kernel,
        grid=(m // tm, dff // tf),
        in_specs=[
            pl.BlockSpec((tm, d), lambda i, f: (i, 0)),
            pl.BlockSpec((1, d), lambda i, f: (0, 0)),
            pl.BlockSpec((d, tf), lambda i, f: (0, f)),
            pl.BlockSpec((d, tf), lambda i, f: (0, f)),
            pl.BlockSpec((tf, d), lambda i, f: (f, 0)),
        ],
        out_specs=pl.BlockSpec((tm, d), lambda i, f: (i, 0)),
        out_shape=jax.ShapeDtypeStruct((m, d), F32),
        scratch_shapes=[pltpu.VMEM((tm, d), BF16), pltpu.VMEM((tm, d), F32)],
        compiler_params=_cparams("parallel", "arbitrary"),
        name="ffn",
    )(h, g.reshape(1, d), wg, wu, wd)


def _pe_kernel(h_ref, p_ref, g_ref, wg_ref, wp_ref, *rest, final):
    o_ref = rest[-1]
    h = h_ref[...]
    ms = jnp.mean(h * h, axis=-1, keepdims=True)
    hn = (h * lax.rsqrt(ms + EPS) * g_ref[...]).astype(BF16)
    gate = jax.nn.sigmoid(jnp.dot(hn, wg_ref[...], preferred_element_type=F32))
    proj = jnp.dot(p_ref[...].astype(BF16), wp_ref[...], preferred_element_type=F32)
    out = h + gate * proj
    if final:
        ms2 = jnp.mean(out * out, axis=-1, keepdims=True)
        out = out * lax.rsqrt(ms2 + EPS) * rest[0][...]
    o_ref[...] = out


def pe_inject(h, p, g, wg, wp, final_gain=None):
    m, d = h.shape
    pd = p.shape[1]
    tm = _pick(m, (512, 256))
    in_specs = [
        pl.BlockSpec((tm, d), lambda i: (i, 0)),
        pl.BlockSpec((tm, pd), lambda i: (i, 0)),
        pl.BlockSpec((1, d), lambda i: (0, 0)),
        pl.BlockSpec((d, d), lambda i: (0, 0)),
        pl.BlockSpec((pd, d), lambda i: (0, 0)),
    ]
    args = [h, p, g.reshape(1, d), wg, wp]
    if final_gain is not None:
        in_specs.append(pl.BlockSpec((1, d), lambda i: (0, 0)))
        args.append(final_gain.reshape(1, d))
    return pl.pallas_call(
        functools.partial(_pe_kernel, final=final_gain is not None),
        grid=(m // tm,),
        in_specs=in_specs,
        out_specs=pl.BlockSpec((tm, d), lambda i: (i, 0)),
        out_shape=jax.ShapeDtypeStruct((m, d), F32),
        compiler_params=_cparams("parallel"),
        name="pe_inject",
    )(*args)


def _flash_kernel(q_ref, k_ref, v_ref, *rest, mode, off, tq, tk, tk_real, dk, dv, hpb, nk, has_bias):
    if has_bias:
        fq_ref, fk_ref, o_ref, m_ref, l_ref, acc_ref = rest
    else:
        o_ref, m_ref, l_ref, acc_ref = rest
    qi = pl.program_id(2)
    ki = pl.program_id(3)

    @pl.when(ki == 0)
    def _():
        m_ref[...] = jnp.full(m_ref.shape, MASK_VALUE, F32)
        l_ref[...] = jnp.zeros(l_ref.shape, F32)
        acc_ref[...] = jnp.zeros(acc_ref.shape, F32)

    q_last = off + qi * tq + (tq - 1)
    if mode == "chunk":
        k_last = (q_last // CHUNK) * CHUNK + (CHUNK - 1)
    else:
        k_last = q_last
    last_tile = jnp.minimum(k_last // tk, nk - 1)

    @pl.when(ki <= last_tile)
    def _():
        qpos = off + qi * tq + lax.broadcasted_iota(jnp.int32, (tq, tk), 0)
        kpos = ki * tk + lax.broadcasted_iota(jnp.int32, (tq, tk), 1)
        if mode == "chunk":
            allowed = (kpos // CHUNK) <= (qpos // CHUNK)
        else:
            allowed = kpos <= qpos
        if tk_real < nk * tk:
            allowed = jnp.logical_and(allowed, kpos < tk_real)
        v = v_ref[0]
        lane = lax.broadcasted_iota(jnp.int32, (tq, hpb * dv), 1)
        pv_full = None
        alpha_full = None
        for hh in range(hpb):
            q = q_ref[0, :, hh * dk:(hh + 1) * dk]
            k = k_ref[0, :, hh * dk:(hh + 1) * dk]
            s = lax.dot_general(q, k, (((1,), (1,)), ((), ())), preferred_element_type=F32)
            if has_bias:
                s = s + fq_ref[0, hh] - fk_ref[0, hh]
            s = jnp.where(allowed, s, MASK_VALUE)
            m_prev = m_ref[hh]
            l_prev = l_ref[hh]
            m_cur = jnp.max(s, axis=1, keepdims=True)
            m_next = jnp.maximum(m_prev, m_cur)
            alpha = jnp.exp(m_prev - m_next)
            p = jnp.exp(s - m_next[:, 0:1])
            l_ref[hh] = alpha * l_prev + jnp.sum(p, axis=1, keepdims=True)
            m_ref[hh] = m_next
            pv = jnp.dot(p.astype(BF16), v, preferred_element_type=F32)
            a_b = alpha[:, 0:1]
            if hpb == 1:
                pv_full, alpha_full = pv, a_b
            elif hh == 0:
                pv_full = pv
                alpha_full = jnp.broadcast_to(a_b, pv.shape)
            else:
                sel = lane >= hh * dv
                pv_full = jnp.where(sel, pv, pv_full)
                alpha_full = jnp.where(sel, a_b, alpha_full)
        acc_ref[...] = acc_ref[...] * alpha_full + pv_full

    @pl.when(ki == nk - 1)
    def _():
        if hpb == 1:
            l_full = l_ref[0][:, 0:1]
        else:
            lane = lax.broadcasted_iota(jnp.int32, (tq, hpb * dv), 1)
            l_full = jnp.broadcast_to(l_ref[0][:, 0:1], (tq, hpb * dv))
            for hh in range(1, hpb):
                l_full = jnp.where(lane >= hh * dv, l_ref[hh][:, 0:1], l_full)
        o_ref[0] = (acc_ref[...] / l_full).astype(o_ref.dtype)


def flash_attention(q, k, v, *, mode, heads, dk, dv, hpb, tk_real, out_dtype, name, fq=None, fk=None):
    b, tq_all, _ = q.shape
    tkp = k.shape[1]
    off = tk_real - tq_all
    tq = _pick(tq_all, (512, 256, 128))
    tk = _pick(tkp, (512, 256, 128))
    nq, nk = tq_all // tq, tkp // tk

    def last_tile(qi):
        q_last = off + qi * tq + (tq - 1)
        if mode == "chunk":
            k_last = (q_last // CHUNK) * CHUNK + (CHUNK - 1)
        else:
            k_last = q_last
        return jnp.minimum(k_last // tk, nk - 1)

    has_bias = fq is not None
    kern = functools.partial(_flash_kernel, mode=mode, off=off, tq=tq, tk=tk, tk_real=tk_real,
                             dk=dk, dv=dv, hpb=hpb, nk=nk, has_bias=has_bias)
    in_specs = [
        pl.BlockSpec((1, tq, hpb * dk), lambda bi, hi, qi, ki: (bi, qi, hi)),
        pl.BlockSpec((1, tk, hpb * dk), lambda bi, hi, qi, ki: (bi, jnp.minimum(ki, last_tile(qi)), hi)),
        pl.BlockSpec((1, tk, hpb * dv), lambda bi, hi, qi, ki: (bi, jnp.minimum(ki, last_tile(qi)), hi)),
    ]
    args = [q, k, v]
    if has_bias:
        in_specs += [
            pl.BlockSpec((1, hpb, tq, 1), lambda bi, hi, qi, ki: (bi, hi, qi, 0)),
            pl.BlockSpec((1, hpb, 1, tk), lambda bi, hi, qi, ki: (bi, hi, 0, jnp.minimum(ki, last_tile(qi)))),
        ]
        args += [fq, fk]
    return pl.pallas_call(
        kern,
        grid=(b, heads // hpb, nq, nk),
        in_specs=in_specs,
        out_specs=pl.BlockSpec((1, tq, hpb * dv), lambda bi, hi, qi, ki: (bi, qi, hi)),
        out_shape=jax.ShapeDtypeStruct((b, tq_all, heads * dv), out_dtype),
        scratch_shapes=[
            pltpu.VMEM((hpb, tq, LANES), F32),
            pltpu.VMEM((hpb, tq, LANES), F32),
            pltpu.VMEM((tq, hpb * dv), F32),
        ],
        compiler_params=_cparams("parallel", "parallel", "parallel", "arbitrary"),
        name=name,
    )(*args)


def _retention_kernel(lg_ref, q_ref, k_ref, v_ref, s0_ref, o_ref, s_ref, state_ref, *, chunk, nc):
    h = pl.program_id(1)
    c = pl.program_id(2)

    @pl.when(c == 0)
    def _():
        state_ref[...] = s0_ref[0, 0].astype(F32)

    lg = lg_ref[h]
    row = lax.broadcasted_iota(jnp.int32, (chunk, chunk), 0)
    col = lax.broadcasted_iota(jnp.int32, (chunk, chunk), 1)
    diff = row - col
    dmask = jnp.where(diff >= 0, jnp.exp(lg * jnp.maximum(diff, 0).astype(F32)), 0.0)
    idx = lax.broadcasted_iota(jnp.int32, (chunk, 1), 0).astype(F32)
    q_dec = jnp.exp(lg * (idx + 1.0))
    k_dec = jnp.exp(lg * (chunk - 1.0 - idx))
    c_dec = jnp.exp(lg * jnp.full((1, RET_DV), float(chunk), F32))

    q = q_ref[0]
    k = k_ref[0]
    v = v_ref[0]
    state = state_ref[...]
    att = lax.dot_general(q, k, (((1,), (1,)), ((), ())), preferred_element_type=F32) * dmask
    qd = (q.astype(F32) * q_dec).astype(BF16)
    o = jnp.dot(att.astype(BF16), v, preferred_element_type=F32)
    o = o + jnp.dot(qd, state.astype(BF16), preferred_element_type=F32)
    o_ref[0] = o.astype(o_ref.dtype)
    kd = (k.astype(F32) * k_dec).astype(BF16)
    upd = lax.dot_general(kd, v, (((0,), (0,)), ((), ())), preferred_element_type=F32)
    new_state = state * c_dec + upd
    state_ref[...] = new_state

    @pl.when(c == nc - 1)
    def _():
        s_ref[0, 0] = new_state.astype(s_ref.dtype)


def retention(q, k, v, s0, lg, chunk):
    b, t, _ = q.shape
    nc = t // chunk
    kern = functools.partial(_retention_kernel, chunk=chunk, nc=nc)
    return pl.pallas_call(
        kern,
        grid_spec=pltpu.PrefetchScalarGridSpec(
            num_scalar_prefetch=1,
            grid=(b, RET_HEADS, nc),
            in_specs=[
                pl.BlockSpec((1, chunk, RET_DK), lambda bi, hi, ci, lg: (bi, ci, hi)),
                pl.BlockSpec((1, chunk, RET_DK), lambda bi, hi, ci, lg: (bi, ci, hi)),
                pl.BlockSpec((1, chunk, RET_DV), lambda bi, hi, ci, lg: (bi, ci, hi)),
                pl.BlockSpec((1, 1, RET_DK, RET_DV), lambda bi, hi, ci, lg: (bi, hi, 0, 0)),
            ],
            out_specs=[
                pl.BlockSpec((1, chunk, RET_DV), lambda bi, hi, ci, lg: (bi, ci, hi)),
                pl.BlockSpec((1, 1, RET_DK, RET_DV), lambda bi, hi, ci, lg: (bi, hi, 0, 0)),
            ],
            scratch_shapes=[pltpu.VMEM((RET_DK, RET_DV), F32)],
        ),
        out_shape=[
            jax.ShapeDtypeStruct((b, t, RET_VW), F32),
            jax.ShapeDtypeStruct((b, RET_HEADS, RET_DK, RET_DV), F32),
        ],
        compiler_params=_cparams("parallel", "parallel", "arbitrary"),
        name="retention",
    )(lg, q, k, v, s0)


def _rms(x, g):
    return x * lax.rsqrt(jnp.mean(x * x, axis=-1, keepdims=True) + EPS) * g


def _rope_tables(pos, half):
    inv = ROPE_THETA ** (-jnp.arange(half, dtype=F32) / half)
    ang = pos.astype(F32)[:, None] * inv[None, :]
    return jnp.cos(ang), jnp.sin(ang)


def _rope(x, cos, sin):
    half = x.shape[-1] // 2
    x1, x2 = x[..., :half], x[..., half:]
    c = cos[None, :, None, :]
    s = sin[None, :, None, :]
    return jnp.concatenate([x1 * c - x2 * s, x2 * c + x1 * s], axis=-1)


def _pad_rows(a, rows):
    pad = rows - a.shape[1]
    if pad == 0:
        return a
    return jnp.pad(a, ((0, 0), (0, pad)) + ((0, 0),) * (a.ndim - 2))


def _retention_mixer(h2, b, t, pos, s0, gmix, w_in, gn, w_out):
    proj = norm_matmul(h2, gmix, w_in, name="ret_in")
    q, k, v, g = jnp.split(proj, [RET_QK, 2 * RET_QK, 2 * RET_QK + RET_VW], axis=-1)
    cos, sin = _rope_tables(pos, RET_DK // 2)
    q = _rope(q.reshape(b, t, RET_HEADS, RET_DK), cos, sin)
    k = _rope(k.reshape(b, t, RET_HEADS, RET_DK), cos, sin) * (RET_DK ** -0.5)
    lg = jnp.log1p(-jnp.exp2(-5.0 - jnp.arange(RET_HEADS, dtype=F32)))
    chunk = _pick(t, (256, 128, 64))
    o, s = retention(q.reshape(b, t, RET_QK).astype(BF16), k.reshape(b, t, RET_QK).astype(BF16),
                     v.reshape(b, t, RET_VW).astype(BF16), s0, lg, chunk)
    o = o.reshape(b, t, RET_HEADS, RET_DV)
    mu = jnp.mean(o, axis=-1, keepdims=True)
    var = jnp.mean(jnp.square(o - mu), axis=-1, keepdims=True)
    y = ((o - mu) * lax.rsqrt(var + EPS)).reshape(b * t, RET_VW) * gn
    a = (jax.nn.silu(g) * y).astype(BF16)
    return matmul(a, w_out, res=h2, name="ret_out"), s


def _mla_mixer(h2, b, t, pos, lat_past, kr_past, gmix, w_in, q_norm, kv_norm, w_qb, w_kvb,
               gq_nope, gq_rope, gk_nope, gk_rope, w_out):
    proj = norm_matmul(h2, gmix, w_in, name="mla_in")
    cq = proj[:, :MLA_Q_LORA]
    ckv = proj[:, MLA_Q_LORA:MLA_Q_LORA + MLA_KV_LORA]
    kr = proj[:, MLA_Q_LORA + MLA_KV_LORA:MLA_Q_LORA + MLA_KV_LORA + MLA_ROPE]
    qf = norm_matmul(cq, q_norm, w_qb, name="mla_qb").reshape(b, t, MLA_HEADS, MLA_NOPE + MLA_ROPE)
    cos, sin = _rope_tables(pos, MLA_ROPE // 2)
    scale = (MLA_NOPE + MLA_ROPE) ** -0.5
    q_nope = _rms(qf[..., :MLA_NOPE], gq_nope) * scale
    q_rope = _rope(_rms(qf[..., MLA_NOPE:], gq_rope), cos, sin) * scale
    q_pad = jnp.concatenate(
        [q_nope, q_rope, jnp.zeros((b, t, MLA_HEADS, MLA_DK_PAD - MLA_NOPE - MLA_ROPE), F32)],
        axis=-1).astype(BF16).reshape(b, t, MLA_HEADS * MLA_DK_PAD)
    lat = _rms(ckv, kv_norm).reshape(b, t, MLA_KV_LORA)
    kr = _rope(_rms(kr, gk_rope).reshape(b, t, 1, MLA_ROPE), cos, sin)[:, :, 0, :]
    lat_all = jnp.concatenate([lat_past, lat], axis=1)
    kr_all = jnp.concatenate([kr_past, kr], axis=1)
    tk = lat_all.shape[1]
    tkp = -(-tk // LANES) * LANES
    lat_all = _pad_rows(lat_all, tkp)
    kr_all = _pad_rows(kr_all, tkp)
    kv = matmul(lat_all.reshape(b * tkp, MLA_KV_LORA).astype(BF16), w_kvb, name="mla_kvb")
    kv = kv.reshape(b, tkp, MLA_HEADS, MLA_NOPE + MLA_V)
    k_nope = _rms(kv[..., :MLA_NOPE], gk_nope)
    v = kv[..., MLA_NOPE:].astype(BF16).reshape(b, tkp, MLA_HEADS * MLA_V)
    k_pad = jnp.concatenate(
        [k_nope, jnp.broadcast_to(kr_all[:, :, None, :], (b, tkp, MLA_HEADS, MLA_ROPE)),
         jnp.zeros((b, tkp, MLA_HEADS, MLA_DK_PAD - MLA_NOPE - MLA_ROPE), F32)],
        axis=-1).astype(BF16).reshape(b, tkp, MLA_HEADS * MLA_DK_PAD)
    o = flash_attention(q_pad, k_pad, v, mode="chunk", heads=MLA_HEADS, dk=MLA_DK_PAD, dv=MLA_V,
                        hpb=1, tk_real=tk, out_dtype=BF16, name="mla_attn")
    return matmul(o.reshape(b * t, MLA_HEADS * MLA_V), w_out, res=h2, name="mla_out"), lat, kr


def _fox_mixer(h2, b, t, k_past, v_past, lf_past, gmix, w_in, b_f, gq, gk, w_out):
    proj = norm_matmul(h2, gmix, w_in, name="fox_in")
    q = proj[:, :FOX_W].reshape(b, t, FOX_HEADS, FOX_DH)
    k = proj[:, FOX_W:2 * FOX_W].reshape(b, t, FOX_HEADS, FOX_DH)
    v = proj[:, 2 * FOX_W:3 * FOX_W].reshape(b, t, FOX_HEADS, FOX_DH)
    g = proj[:, 3 * FOX_W:4 * FOX_W]
    fl = proj[:, 4 * FOX_W:4 * FOX_W + FOX_HEADS].reshape(b, t, FOX_HEADS)
    q = _rms(q, gq)
    k = _rms(k, gk)
    lf = jax.nn.log_sigmoid(fl + b_f)
    k_all = jnp.concatenate([k_past, k], axis=1)
    v_all = jnp.concatenate([v_past, v], axis=1)
    lf_all = jnp.concatenate([lf_past.astype(F32), lf], axis=1)
    fcum = jnp.cumsum(lf_all, axis=1)
    tk = k_all.shape[1]
    tkp = -(-tk // LANES) * LANES
    fq = fcum[:, tk - t:].transpose(0, 2, 1)[..., None]
    fk = _pad_rows(fcum, tkp).transpose(0, 2, 1)[:, :, None, :]
    zpad_q = jnp.zeros((b, t, FOX_HEADS, FOX_DK_PAD - FOX_DH), BF16)
    zpad_k = jnp.zeros((b, tk, FOX_HEADS, FOX_DK_PAD - FOX_DH), BF16)
    q_pad = jnp.concatenate([(q * (FOX_DH ** -0.5)).astype(BF16), zpad_q],
                            axis=-1).reshape(b, t, FOX_HEADS * FOX_DK_PAD)
    k_pad = jnp.concatenate([k_all.astype(BF16), zpad_k], axis=-1).reshape(b, tk, FOX_HEADS * FOX_DK_PAD)
    k_pad = _pad_rows(k_pad, tkp)
    v_bf = _pad_rows(v_all.astype(BF16).reshape(b, tk, FOX_W), tkp)
    o = flash_attention(q_pad, k_pad, v_bf, mode="token", heads=FOX_HEADS, dk=FOX_DK_PAD, dv=FOX_DH,
                        hpb=2, tk_real=tk, out_dtype=F32, name="fox_attn", fq=fq, fk=fk)
    a = (jax.nn.sigmoid(g) * o.reshape(b * t, FOX_W)).astype(BF16)
    return matmul(a, w_out, res=h2, name="fox_out"), k, v, lf


def _trunk(x, p, past_len, ret_states, lat_past, kr_past, fk_past, fv_past, flf_past, w):
    b, t, d = x.shape
    pos = past_len + jnp.arange(t)
    h2 = x.reshape(b * t, d)
    new_ret, new_lat, new_kr, new_fk, new_fv, new_flf = [], [], [], [], [], []
    for i in range(DEPTH):
        kind, j = i % N_MIXERS, i // N_MIXERS
        if kind == 0:
            h2, s = _retention_mixer(h2, b, t, pos, ret_states[j], w['norm_mix'][i], w['ret_w_in'][j],
                                     w['ret_gn'][j], w['ret_w_out'][j])
            new_ret.append(s)
        elif kind == 1:
            h2, lat, kr = _mla_mixer(h2, b, t, pos, lat_past[j], kr_past[j], w['norm_mix'][i],
                                     w['mla_w_in'][j], w['mla_q_norm'][j], w['mla_kv_norm'][j],
                                     w['mla_w_qb'][j], w['mla_w_kvb'][j], w['mla_gq_nope'][j],
                                     w['mla_gq_rope'][j], w['mla_gk_nope'][j], w['mla_gk_rope'][j],
                                     w['mla_w_out'][j])
            new_lat.append(lat)
            new_kr.append(kr)
        else:
            h2, fk, fv, flf = _fox_mixer(h2, b, t, fk_past[j], fv_past[j], flf_past[j], w['norm_mix'][i],
                                         w['fox_w_in'][j], w['fox_b_f'][j], w['fox_gq'][j], w['fox_gk'][j],
                                         w['fox_w_out'][j])
            new_fk.append(fk)
            new_fv.append(fv)
            new_flf.append(flf)
        h2 = ffn(h2, w['norm_ffn'][i], w['ffn_w_gate'][i], w['ffn_w_up'][i], w['ffn_w_down'][i])
        h2 = pe_inject(h2, p[i].reshape(b * t, PE_DIM), w['norm_pe'][i], w['pe_w_gate'][i], w['pe_w_proj'][i],
                       final_gain=w['norm_final'] if i == DEPTH - 1 else None)
    return (h2.reshape(b, t, d), jnp.stack(new_ret), jnp.stack(new_lat), jnp.stack(new_kr),
            jnp.stack(new_fk), jnp.stack(new_fv), jnp.stack(new_flf))


def _pad_cols(wm, n):
    return jnp.pad(wm, ((0, 0), (0, 0), (0, n - wm.shape[-1])))


def kernel(x_prompt, x_sample, state_ret, cache_mla_latent, cache_mla_krope, cache_fox_k, cache_fox_v,
           cache_fox_logf, p_prompt, p_sample, norm_mix, norm_ffn, norm_pe, norm_final, ret_w_in, ret_gn,
           ret_w_out, mla_w_in, mla_q_norm, mla_kv_norm, mla_w_qb, mla_w_kvb, mla_gq_nope, mla_gq_rope,
           mla_gk_nope, mla_gk_rope, mla_w_out, fox_w_in, fox_b_f, fox_gq, fox_gk, fox_w_out, ffn_w_gate,
           ffn_w_up, ffn_w_down, pe_w_proj, pe_w_gate):
    bf = lambda a: a.astype(BF16)
    w = dict(norm_mix=norm_mix, norm_ffn=norm_ffn, norm_pe=norm_pe, norm_final=norm_final,
             ret_w_in=bf(ret_w_in), ret_gn=ret_gn, ret_w_out=bf(ret_w_out),
             mla_w_in=bf(_pad_cols(mla_w_in, 896)), mla_q_norm=mla_q_norm, mla_kv_norm=mla_kv_norm,
             mla_w_qb=bf(mla_w_qb), mla_w_kvb=bf(mla_w_kvb), mla_gq_nope=mla_gq_nope,
             mla_gq_rope=mla_gq_rope, mla_gk_nope=mla_gk_nope, mla_gk_rope=mla_gk_rope,
             mla_w_out=bf(mla_w_out),
             fox_w_in=bf(_pad_cols(fox_w_in, 4224)), fox_b_f=fox_b_f, fox_gq=fox_gq, fox_gk=fox_gk,
             fox_w_out=bf(fox_w_out),
             ffn_w_gate=bf(ffn_w_gate), ffn_w_up=bf(ffn_w_up), ffn_w_down=bf(ffn_w_down),
             pe_w_proj=bf(pe_w_proj), pe_w_gate=bf(pe_w_gate))
    bp = x_prompt.shape[0]
    dt = x_prompt.dtype
    n_ret, n_mla, n_fox = state_ret.shape[0], cache_mla_latent.shape[0], cache_fox_k.shape[0]
    y_p, ret_p, lat_p, kr_p, fk_p, fv_p, flf_p = _trunk(
        x_prompt, p_prompt, 0,
        jnp.zeros((n_ret, bp, RET_HEADS, RET_DK, RET_DV), dt),
        jnp.zeros((n_mla, bp, 0, MLA_KV_LORA), dt), jnp.zeros((n_mla, bp, 0, MLA_ROPE), dt),
        jnp.zeros((n_fox, bp, 0, FOX_HEADS, FOX_DH), dt), jnp.zeros((n_fox, bp, 0, FOX_HEADS, FOX_DH), dt),
        jnp.zeros((n_fox, bp, 0, FOX_HEADS), dt), w)
    y_s, ret_s, lat_s, kr_s, fk_s, fv_s, flf_s = _trunk(
        x_sample, p_sample, cache_mla_latent.shape[2], state_ret, cache_mla_latent, cache_mla_krope,
        cache_fox_k, cache_fox_v, cache_fox_logf, w)
    return (y_p, y_s, ret_p, ret_s, lat_p, kr_p, lat_s, kr_s, fk_p, fv_p, flf_p, fk_s, fv_s, flf_s)
```

```python
import functools
import math

import jax
import jax.numpy as jnp
from jax import lax
from jax.experimental import pallas as pl
from jax.experimental.pallas import tpu as pltpu

D_MODEL = 1024
DEPTH = 4
CHUNK = 64
PE_DIM = 256
N_MIXERS = 3
EPS = 1e-6
ROPE_THETA = 10000.0
RET_HEADS = 4
RET_DK = 256
RET_DV = 512
RET_QK = RET_HEADS * RET_DK
RET_VW = RET_HEADS * RET_DV
MLA_HEADS = 16
MLA_NOPE = 128
MLA_ROPE = 64
MLA_V = 128
MLA_Q_LORA = 512
MLA_KV_LORA = 256
FOX_HEADS = 16
FOX_DH = 64
FOX_W = FOX_HEADS * FOX_DH
D_FF = ((8 * D_MODEL + 3 * 256 - 1) // (3 * 256)) * 256

LANES = 128
MLA_DK_PAD = 2 * LANES
MLA_IN_PAD = 7 * LANES
PROJ_TN = 512
MASK_VALUE = -1e30
LOG2E = math.log2(math.e)
VMEM_LIMIT = 48 * 1024 * 1024

F32 = jnp.float32
BF16 = jnp.bfloat16
NT_DIMS = (((1,), (1,)), ((), ()))


def _cparams(*sem):
    return pltpu.CompilerParams(dimension_semantics=sem, vmem_limit_bytes=VMEM_LIMIT)


def _pick(n, prefs):
    for p in prefs:
        if n % p == 0:
            return p
    return n


def _rms_bf16(x, g):
    ms = jnp.mean(x * x, axis=-1, keepdims=True)
    return (x * lax.rsqrt(ms + EPS) * g).astype(BF16)


def _pos_table_spec(t, tm):
    if t % tm == 0:
        nt = t // tm
        return pl.BlockSpec((tm, LANES), lambda i, *_: (i % nt, 0))
    return pl.BlockSpec((tm, LANES), lambda i, *_: (0, 0))


def _pos_table(tab, t, tm):
    return tab if t % tm == 0 else jnp.tile(tab, (tm // t, 1))


def _norm_mm_kernel(x_ref, g_ref, w_ref, o_ref, xn_ref):
    @pl.when(pl.program_id(1) == 0)
    def _():
        xn_ref[...] = _rms_bf16(x_ref[...], g_ref[...])

    o_ref[...] = jnp.dot(xn_ref[...], w_ref[...], preferred_element_type=F32).astype(o_ref.dtype)


def norm_matmul(x, g, w, out_dtype=F32, name="norm_mm"):
    m, k = x.shape
    n = w.shape[1]
    tm = _pick(m, (1024, 512, 256))
    tn = _pick(n, (1024, 768, 512, 896, 128))
    return pl.pallas_call(
        _norm_mm_kernel,
        grid=(m // tm, n // tn),
        in_specs=[
            pl.BlockSpec((tm, k), lambda i, j: (i, 0)),
            pl.BlockSpec((1, k), lambda i, j: (0, 0)),
            pl.BlockSpec((k, tn), lambda i, j: (0, j)),
        ],
        out_specs=pl.BlockSpec((tm, tn), lambda i, j: (i, j)),
        out_shape=jax.ShapeDtypeStruct((m, n), out_dtype),
        scratch_shapes=[pltpu.VMEM((tm, k), BF16)],
        compiler_params=_cparams("parallel", "arbitrary"),
        name=name,
    )(x, g.reshape(1, k), w)


def _mm_res_kernel(a_ref, w_ref, r_ref, o_ref):
    o_ref[...] = r_ref[...] + jnp.dot(a_ref[...], w_ref[...], preferred_element_type=F32)


def matmul_res(a, w, res, name="mm"):
    m, k = a.shape
    n = w.shape[1]
    tm = _pick(m, (1024, 512, 256))
    tn = _pick(n, (1024, 512, 128))
    return pl.pallas_call(
        _mm_res_kernel,
        grid=(m // tm, n // tn),
        in_specs=[
            pl.BlockSpec((tm, k), lambda i, j: (i, 0)),
            pl.BlockSpec((k, tn), lambda i, j: (0, j)),
            pl.BlockSpec((tm, tn), lambda i, j: (i, j)),
        ],
        out_specs=pl.BlockSpec((tm, tn), lambda i, j: (i, j)),
        out_shape=jax.ShapeDtypeStruct((m, n), F32),
        compiler_params=_cparams("parallel", "arbitrary"),
        name=name,
    )(a, w, res)


def _ffn_kernel(x_ref, g_ref, wg_ref, wu_ref, wd_ref, o_ref, xn_ref, acc_ref):
    f = pl.program_id(1)

    @pl.when(f == 0)
    def _():
        x = x_ref[...]
        xn_ref[...] = _rms_bf16(x, g_ref[...])
        acc_ref[...] = x

    xn = xn_ref[...]
    gate = jnp.dot(xn, wg_ref[...], preferred_element_type=F32)
    up = jnp.dot(xn, wu_ref[...], preferred_element_type=F32)
    act = (gate * jax.nn.sigmoid(gate) * up).astype(BF16)
    acc_ref[...] += jnp.dot(act, wd_ref[...], preferred_element_type=F32)

    @pl.when(f == pl.num_programs(1) - 1)
    def _():
        o_ref[...] = acc_ref[...]


def ffn(h, g, wg, wu, wd):
    m, d = h.shape
    dff = wg.shape[1]
    tm = _pick(m, (1024, 512))
    tf = 256
    return pl.pallas_call(
        _ffn_kernel,
        grid=(m // tm, dff // tf),
        in_specs=[
            pl.BlockSpec((tm, d), lambda i, f: (i, 0)),
            pl.BlockSpec((1, d), lambda i, f: (0, 0)),
            pl.BlockSpec((d, tf), lambda i, f: (0, f)),
            pl.BlockSpec((d, tf), lambda i, f: (0, f)),
            pl.BlockSpec((tf, d), lambda i, f: (f, 0)),
        ],
        out_specs=pl.BlockSpec((tm, d), lambda i, f: (i, 0)),
        out_shape=jax.ShapeDtypeStruct((m, d), F32),
        scratch_shapes=[pltpu.VMEM((tm, d), BF16), pltpu.VMEM((tm, d), F32)],
        compiler_params=_cparams("parallel", "arbitrary"),
        name="ffn",
    )(h, g.reshape(1, d), wg, wu, wd)


def _pe_kernel(h_ref, p_ref, g_ref, wg_ref, wp_ref, *rest, final):
    o_ref = rest[-1]
    h = h_ref[...]
    hn = _rms_bf16(h, g_ref[...])
    gate = jax.nn.sigmoid(jnp.dot(hn, wg_ref[...], preferred_element_type=F32))
    proj = jnp.dot(p_ref[...].astype(BF16), wp_ref[...], preferred_element_type=F32)
    out = h + gate * proj
    if final:
        ms2 = jnp.mean(out * out, axis=-1, keepdims=True)
        out = out * lax.rsqrt(ms2 + EPS) * rest[0][...]
    o_ref[...] = out


def pe_inject(h, p, g, wg, wp, final_gain=None):
    m, d = h.shape
    pd = p.shape[1]
    tm = _pick(m, (512, 256))
    in_specs = [
        pl.BlockSpec((tm, d), lambda i: (i, 0)),
        pl.BlockSpec((tm, pd), lambda i: (i, 0)),
        pl.BlockSpec((1, d), lambda i: (0, 0)),
        pl.BlockSpec((d, d), lambda i: (0, 0)),
        pl.BlockSpec((pd, d), lambda i: (0, 0)),
    ]
    args = [h, p, g.reshape(1, d), wg, wp]
    if final_gain is not None:
        in_specs.append(pl.BlockSpec((1, d), lambda i: (0, 0)))
        args.append(final_gain.reshape(1, d))
    return pl.pallas_call(
        functools.partial(_pe_kernel, final=final_gain is not None),
        grid=(m // tm,),
        in_specs=in_specs,
        out_specs=pl.BlockSpec((tm, d), lambda i: (i, 0)),
        out_shape=jax.ShapeDtypeStruct((m, d), F32),
        compiler_params=_cparams("parallel"),
        name="pe_inject",
    )(*args)


def _attn_kernel(*refs, tq, nq, tkp, mw, hpb, has_bias, has_gate):
    q_ref, k_ref, v_ref, mb_ref = refs[:4]
    pos = 4
    if has_bias:
        fq_ref, fk_ref = refs[4:6]
        pos = 6
    if has_gate:
        g_ref = refs[pos]
        pos += 1
    o_ref = refs[pos]
    bw_k = q_ref.shape[2]
    bw_v = v_ref.shape[2]
    mb = mb_ref[...]
    lane_k = lax.broadcasted_iota(jnp.int32, (tq, bw_k), 1)
    lane_v = lax.broadcasted_iota(jnp.int32, (tq, bw_v), 1)

    for qi in range(nq):
        r0 = qi * tq
        kv_len = tkp - (nq - 1 - qi) * tq
        head_len = kv_len - mw
        q_blk = q_ref[0, r0:r0 + tq, :]
        out = None
        for hh in range(hpb):
            q = q_blk if hpb == 1 else jnp.where(lane_k // (bw_k // hpb) == hh, q_blk, jnp.zeros_like(q_blk))

            def scores(lo, hi):
                s = lax.dot_general(q, k_ref[0, lo:hi, :], NT_DIMS, preferred_element_type=F32)
                if has_bias:
                    s = s - fk_ref[0, hh, :, lo:hi]
                return s

            s_tail = scores(head_len, kv_len) + mb
            m = jnp.max(s_tail, axis=1, keepdims=True)
            if head_len > 0:
                s_head = scores(0, head_len)
                m = jnp.maximum(m, jnp.max(s_head, axis=1, keepdims=True))
            if has_bias:
                fq = fq_ref[0, hh, r0:r0 + tq, :]
                shift = fq - (fq + m)
            else:
                shift = -m
            p_tail = jnp.exp2(s_tail + shift)
            l = jnp.sum(p_tail, axis=1, keepdims=True)
            pv = jnp.dot(p_tail.astype(BF16), v_ref[0, head_len:kv_len, :], preferred_element_type=F32)
            if head_len > 0:
                p_head = jnp.exp2(s_head + shift)
                l = l + jnp.sum(p_head, axis=1, keepdims=True)
                pv = pv + jnp.dot(p_head.astype(BF16), v_ref[0, 0:head_len, :], preferred_element_type=F32)
            o_h = pv * (1.0 / l)
            out = o_h if hh == 0 else jnp.where(lane_v // (bw_v // hpb) == hh, o_h, out)
        if has_gate:
            out = jax.nn.sigmoid(g_ref[0, r0:r0 + tq, :].astype(F32)) * out
        o_ref[0, r0:r0 + tq, :] = out.astype(o_ref.dtype)


def attention(q, k, v, *, mode, n_blocks, hpb, tk_real, name, fq=None, fk=None, gate=None):
    b, tq_all, wq = q.shape
    tkp = k.shape[1]
    bw_k = wq // n_blocks
    bw_v = v.shape[2] // n_blocks
    tq = _pick(tq_all, (512,))
    nq = tq_all // tq
    mw = tq if nq > 1 or tq_all == tkp else LANES
    q0 = tk_real - tq
    k0 = tkp - mw
    qpos = q0 + lax.broadcasted_iota(jnp.int32, (tq, mw), 0)
    kpos = k0 + lax.broadcasted_iota(jnp.int32, (tq, mw), 1)
    allowed = (kpos // CHUNK <= qpos // CHUNK) if mode == "chunk" else (kpos <= qpos)
    allowed = jnp.logical_and(allowed, kpos < tk_real)
    mb = jnp.where(allowed, 0.0, MASK_VALUE).astype(F32)

    in_specs = [
        pl.BlockSpec((1, tq_all, bw_k), lambda bi, hi: (bi, 0, hi)),
        pl.BlockSpec((1, tkp, bw_k), lambda bi, hi: (bi, 0, hi)),
        pl.BlockSpec((1, tkp, bw_v), lambda bi, hi: (bi, 0, hi)),
        pl.BlockSpec((tq, mw), lambda bi, hi: (0, 0)),
    ]
    args = [q, k, v, mb]
    if fq is not None:
        in_specs += [
            pl.BlockSpec((1, hpb, tq_all, 1), lambda bi, hi: (bi, hi, 0, 0)),
            pl.BlockSpec((1, hpb, 1, tkp), lambda bi, hi: (bi, hi, 0, 0)),
        ]
        args += [fq, fk]
    if gate is not None:
        in_specs.append(pl.BlockSpec((1, tq_all, bw_v), lambda bi, hi: (bi, 0, hi)))
        args.append(gate)
    kern = functools.partial(_attn_kernel, tq=tq, nq=nq, tkp=tkp, mw=mw, hpb=hpb,
                             has_bias=fq is not None, has_gate=gate is not None)
    return pl.pallas_call(
        kern,
        grid=(b, n_blocks),
        in_specs=in_specs,
        out_specs=pl.BlockSpec((1, tq_all, bw_v), lambda bi, hi: (bi, 0, hi)),
        out_shape=jax.ShapeDtypeStruct((b, tq_all, n_blocks * bw_v), BF16),
        compiler_params=_cparams("parallel", "parallel"),
        name=name,
    )(*args)


def _ret_in_kernel(x_ref, g_ref, w_ref, cos_ref, sin_ref, q_ref, k_ref, v_ref, gate_ref, xn_ref):
    j = pl.program_id(1)
    nq = RET_QK // PROJ_TN
    nv = RET_VW // PROJ_TN

    @pl.when(j == 0)
    def _():
        xn_ref[...] = _rms_bf16(x_ref[...], g_ref[...])

    acc = jnp.dot(xn_ref[...], w_ref[...], preferred_element_type=F32)

    def rope_store(dst, mult):
        cos = cos_ref[...]
        sin = sin_ref[...]
        half = RET_DK // 2
        for hh in range(PROJ_TN // RET_DK):
            x1 = acc[:, hh * RET_DK:hh * RET_DK + half]
            x2 = acc[:, hh * RET_DK + half:(hh + 1) * RET_DK]
            dst[:, hh * RET_DK:hh * RET_DK + half] = ((x1 * cos - x2 * sin) * mult).astype(BF16)
            dst[:, hh * RET_DK + half:(hh + 1) * RET_DK] = ((x2 * cos + x1 * sin) * mult).astype(BF16)

    @pl.when(j < nq)
    def _():
        rope_store(q_ref, 1.0)

    @pl.when(jnp.logical_and(j >= nq, j < 2 * nq))
    def _():
        rope_store(k_ref, RET_DK ** -0.5)

    @pl.when(jnp.logical_and(j >= 2 * nq, j < 2 * nq + nv))
    def _():
        v_ref[...] = acc.astype(BF16)

    @pl.when(j >= 2 * nq + nv)
    def _():
        gate_ref[...] = acc.astype(BF16)


def ret_in(x, g, w, cos, sin, t):
    m, d = x.shape
    tm = _pick(m, (1024, 512))
    nq = RET_QK // PROJ_TN
    nv = RET_VW // PROJ_TN
    nj = 2 * nq + 2 * nv
    clip = lambda j, lo, n: jnp.clip(j - lo, 0, n - 1)
    tab = _pos_table_spec(t, tm)
    return pl.pallas_call(
        _ret_in_kernel,
        grid=(m // tm, nj),
        in_specs=[
            pl.BlockSpec((tm, d), lambda i, j: (i, 0)),
            pl.BlockSpec((1, d), lambda i, j: (0, 0)),
            pl.BlockSpec((d, PROJ_TN), lambda i, j: (0, j)),
            tab, tab,
        ],
        out_specs=[
            pl.BlockSpec((tm, PROJ_TN), lambda i, j: (i, clip(j, 0, nq))),
            pl.BlockSpec((tm, PROJ_TN), lambda i, j: (i, clip(j, nq, nq))),
            pl.BlockSpec((tm, PROJ_TN), lambda i, j: (i, clip(j, 2 * nq, nv))),
            pl.BlockSpec((tm, PROJ_TN), lambda i, j: (i, clip(j, 2 * nq + nv, nv))),
        ],
        out_shape=[
            jax.ShapeDtypeStruct((m, RET_QK), BF16),
            jax.ShapeDtypeStruct((m, RET_QK), BF16),
            jax.ShapeDtypeStruct((m, RET_VW), BF16),
            jax.ShapeDtypeStruct((m, RET_VW), BF16),
        ],
        scratch_shapes=[pltpu.VMEM((tm, d), BF16)],
        compiler_params=_cparams("parallel", "arbitrary"),
        name="ret_in",
    )(x, g.reshape(1, d), w, _pos_table(cos, t, tm), _pos_table(sin, t, tm))


def _retention_kernel(lg_ref, q_ref, k_ref, v_ref, gate_ref, gn_ref, s0_ref, a_ref, s_ref, state_ref, *, chunk, nc):
    h = pl.program_id(1)
    c = pl.program_id(2)

    @pl.when(c == 0)
    def _():
        state_ref[...] = s0_ref[0, 0].astype(F32)

    lg = lg_ref[h]
    row = lax.broadcasted_iota(jnp.int32, (chunk, chunk), 0)
    col = lax.broadcasted_iota(jnp.int32, (chunk, chunk), 1)
    diff = row - col
    dmask = jnp.where(diff >= 0, jnp.exp(lg * jnp.maximum(diff, 0).astype(F32)), 0.0)
    idx = lax.broadcasted_iota(jnp.int32, (chunk, 1), 0).astype(F32)
    q_dec = jnp.exp(lg * (idx + 1.0))
    k_dec = jnp.exp(lg * (chunk - 1.0 - idx))
    c_dec = jnp.exp(lg * jnp.full((1, RET_DV), float(chunk), F32))

    q = q_ref[0]
    k = k_ref[0]
    v = v_ref[0]
    state = state_ref[...]
    att = lax.dot_general(q, k, NT_DIMS, preferred_element_type=F32) * dmask
    qd = (q.astype(F32) * q_dec).astype(BF16)
    o = jnp.dot(att.astype(BF16), v, preferred_element_type=F32)
    o = o + jnp.dot(qd, state.astype(BF16), preferred_element_type=F32)
    kd = (k.astype(F32) * k_dec).astype(BF16)
    upd = lax.dot_general(kd, v, (((0,), (0,)), ((), ())), preferred_element_type=F32)
    new_state = state * c_dec + upd
    state_ref[...] = new_state

    mu = jnp.mean(o, axis=-1, keepdims=True)
    oc = o - mu
    var = jnp.mean(oc * oc, axis=-1, keepdims=True)
    y = oc * lax.rsqrt(var + EPS) * gn_ref[...]
    g = gate_ref[0].astype(F32)
    a_ref[0] = (g * jax.nn.sigmoid(g) * y).astype(BF16)

    @pl.when(c == nc - 1)
    def _():
        s_ref[0, 0] = new_state


def retention(q, k, v, gate, gn, s0, lg, chunk):
    b, t, _ = q.shape
    nc = t // chunk
    kern = functools.partial(_retention_kernel, chunk=chunk, nc=nc)
    tok = lambda bi, hi, ci, lg: (bi, ci, hi)
    st = lambda bi, hi, ci, lg: (bi, hi, 0, 0)
    return pl.pallas_call(
        kern,
        grid_spec=pltpu.PrefetchScalarGridSpec(
            num_scalar_prefetch=1,
            grid=(b, RET_HEADS, nc),
            in_specs=[
                pl.BlockSpec((1, chunk, RET_DK), tok),
                pl.BlockSpec((1, chunk, RET_DK), tok),
                pl.BlockSpec((1, chunk, RET_DV), tok),
                pl.BlockSpec((1, chunk, RET_DV), tok),
                pl.BlockSpec((1, RET_DV), lambda bi, hi, ci, lg: (0, hi)),
                pl.BlockSpec((1, 1, RET_DK, RET_DV), st),
            ],
            out_specs=[
                pl.BlockSpec((1, chunk, RET_DV), tok),
                pl.BlockSpec((1, 1, RET_DK, RET_DV), st),
            ],
            scratch_shapes=[pltpu.VMEM((RET_DK, RET_DV), F32)],
        ),
        out_shape=[
            jax.ShapeDtypeStruct((b, t, RET_VW), BF16),
            jax.ShapeDtypeStruct((b, RET_HEADS, RET_DK, RET_DV), F32),
        ],
        compiler_params=_cparams("parallel", "parallel", "arbitrary"),
        name="retention",
    )(lg, q, k, v, gate, gn, s0)


def _rope_lanes(x, c_tab, s1_tab, s2_tab):
    return x * c_tab + pltpu.roll(x, 3 * LANES // 4, 1) * s1_tab + pltpu.roll(x, LANES // 4, 1) * s2_tab


def _mla_q_kernel(p_ref, qn_ref, kvn_ref, gqn_ref, gqr_ref, gkr_ref, w_ref, c_ref, s1_ref, s2_ref,
                  q_ref, lat_ref, kr_ref):
    c_tab, s1_tab, s2_tab = c_ref[...], s1_ref[...], s2_ref[...]
    qscale = (MLA_NOPE + MLA_ROPE) ** -0.5 * LOG2E

    ckv = p_ref[:, MLA_Q_LORA:MLA_Q_LORA + MLA_KV_LORA]
    lat_ref[...] = ckv * lax.rsqrt(jnp.mean(ckv * ckv, axis=-1, keepdims=True) + EPS) * kvn_ref[...]

    krp = p_ref[:, MLA_Q_LORA + MLA_KV_LORA:MLA_IN_PAD]
    krn = krp * lax.rsqrt(jnp.sum(krp * krp, axis=-1, keepdims=True) * (1.0 / MLA_ROPE) + EPS) * gkr_ref[...]
    kr_ref[...] = _rope_lanes(krn, c_tab, s1_tab, s2_tab)

    cqn = _rms_bf16(p_ref[:, 0:MLA_Q_LORA], qn_ref[...])
    hg = 4
    for g0 in range(0, MLA_HEADS, hg):
        acc = jnp.dot(cqn, w_ref[:, g0 * MLA_DK_PAD:(g0 + hg) * MLA_DK_PAD], preferred_element_type=F32)
        for hh in range(hg):
            a = acc[:, hh * MLA_DK_PAD:hh * MLA_DK_PAD + LANES]
            r = acc[:, hh * MLA_DK_PAD + LANES:(hh + 1) * MLA_DK_PAD]
            an = a * lax.rsqrt(jnp.mean(a * a, axis=-1, keepdims=True) + EPS) * gqn_ref[...]
            rn = r * lax.rsqrt(jnp.sum(r * r, axis=-1, keepdims=True) * (1.0 / MLA_ROPE) + EPS) * gqr_ref[...]
            col = (g0 + hh) * MLA_DK_PAD
            q_ref[:, col:col + LANES] = (an * qscale).astype(BF16)
            q_ref[:, col + LANES:col + MLA_DK_PAD] = (_rope_lanes(rn, c_tab, s1_tab, s2_tab) * qscale).astype(BF16)


def mla_q(proj, q_norm, kv_norm, gq_nope, gq_rope_pad, gk_rope_pad, w_qb_pad, tabs, t):
    m = proj.shape[0]
    tm = _pick(m, (512, 256))
    row = lambda n: pl.BlockSpec((1, n), lambda i: (0, 0))
    tab = _pos_table_spec(t, tm)
    return pl.pallas_call(
        _mla_q_kernel,
        grid=(m // tm,),
        in_specs=[
            pl.BlockSpec((tm, MLA_IN_PAD), lambda i: (i, 0)),
            row(MLA_Q_LORA), row(MLA_KV_LORA), row(LANES), row(LANES), row(LANES),
            pl.BlockSpec((MLA_Q_LORA, MLA_HEADS * MLA_DK_PAD), lambda i: (0, 0)),
            tab, tab, tab,
        ],
        out_specs=[
            pl.BlockSpec((tm, MLA_HEADS * MLA_DK_PAD), lambda i: (i, 0)),
            pl.BlockSpec((tm, MLA_KV_LORA), lambda i: (i, 0)),
            pl.BlockSpec((tm, LANES), lambda i: (i, 0)),
        ],
        out_shape=[
            jax.ShapeDtypeStruct((m, MLA_HEADS * MLA_DK_PAD), BF16),
            jax.ShapeDtypeStruct((m, MLA_KV_LORA), F32),
            jax.ShapeDtypeStruct((m, LANES), F32),
        ],
        compiler_params=_cparams("parallel"),
        name="mla_q",
    )(proj, q_norm.reshape(1, -1), kv_norm.reshape(1, -1), gq_nope.reshape(1, -1), gq_rope_pad, gk_rope_pad,
      w_qb_pad, *[_pos_table(x, t, tm) for x in tabs])


def _mla_kv_kernel(lat_ref, kr_ref, gk_ref, wk_ref, wv_ref, k_ref, v_ref):
    latb = lat_ref[...].astype(BF16)
    krb = kr_ref[...].astype(BF16)
    v_ref[...] = jnp.dot(latb, wv_ref[...], preferred_element_type=F32).astype(BF16)
    hg = 4
    for g0 in range(0, MLA_HEADS, hg):
        acc = jnp.dot(latb, wk_ref[:, g0 * MLA_NOPE:(g0 + hg) * MLA_NOPE], preferred_element_type=F32)
        for hh in range(hg):
            a = acc[:, hh * MLA_NOPE:(hh + 1) * MLA_NOPE]
            an = a * lax.rsqrt(jnp.mean(a * a, axis=-1, keepdims=True) + EPS) * gk_ref[...]
            col = (g0 + hh) * MLA_DK_PAD
            k_ref[:, col:col + LANES] = an.astype(BF16)
            k_ref[:, col + LANES:col + MLA_DK_PAD] = krb


def mla_kv(lat_all, kr_all, gk_nope, wk, wv):
    m = lat_all.shape[0]
    tm = _pick(m, (512, 256))
    return pl.pallas_call(
        _mla_kv_kernel,
        grid=(m // tm,),
        in_specs=[
            pl.BlockSpec((tm, MLA_KV_LORA), lambda i: (i, 0)),
            pl.BlockSpec((tm, LANES), lambda i: (i, 0)),
            pl.BlockSpec((1, MLA_NOPE), lambda i: (0, 0)),
            pl.BlockSpec((MLA_KV_LORA, MLA_HEADS * MLA_NOPE), lambda i: (0, 0)),
            pl.BlockSpec((MLA_KV_LORA, MLA_HEADS * MLA_V), lambda i: (0, 0)),
        ],
        out_specs=[
            pl.BlockSpec((tm, MLA_HEADS * MLA_DK_PAD), lambda i: (i, 0)),
            pl.BlockSpec((tm, MLA_HEADS * MLA_V), lambda i: (i, 0)),
        ],
        out_shape=[
            jax.ShapeDtypeStruct((m, MLA_HEADS * MLA_DK_PAD), BF16),
            jax.ShapeDtypeStruct((m, MLA_HEADS * MLA_V), BF16),
        ],
        compiler_params=_cparams("parallel"),
        name="mla_kv",
    )(lat_all, kr_all, gk_nope.reshape(1, -1), wk, wv)


def _head_norm(x, gain):
    lane = lax.broadcasted_iota(jnp.int32, (x.shape[0], LANES), 1)
    low = lane < FOX_DH
    outs = []
    for c in range(x.shape[1] // LANES):
        xb = x[:, c * LANES:(c + 1) * LANES]
        sq = xb * xb
        s_lo = jnp.sum(jnp.where(low, sq, 0.0), axis=-1, keepdims=True)
        s_hi = jnp.sum(jnp.where(low, 0.0, sq), axis=-1, keepdims=True)
        ms = jnp.where(low, s_lo, s_hi) * (1.0 / FOX_DH)
        outs.append(xb * lax.rsqrt(ms + EPS) * gain)
    return outs


def _fox_in_kernel(x_ref, g_ref, w_ref, gq_ref, gk_ref, bf_ref,
                   q_ref, k_ref, kb_ref, v_ref, vb_ref, gate_ref, lf_ref, xn_ref):
    j = pl.program_id(1)
    ns = FOX_W // PROJ_TN

    @pl.when(j == 0)
    def _():
        xn_ref[...] = _rms_bf16(x_ref[...], g_ref[...])

    acc = jnp.dot(xn_ref[...], w_ref[...], preferred_element_type=F32)

    @pl.when(j < ns)
    def _():
        for c, blk in enumerate(_head_norm(acc, gq_ref[...])):
            q_ref[:, c * LANES:(c + 1) * LANES] = (blk * (FOX_DH ** -0.5 * LOG2E)).astype(BF16)

    @pl.when(jnp.logical_and(j >= ns, j < 2 * ns))
    def _():
        for c, blk in enumerate(_head_norm(acc, gk_ref[...])):
            k_ref[:, c * LANES:(c + 1) * LANES] = blk
            kb_ref[:, c * LANES:(c + 1) * LANES] = blk.astype(BF16)

    @pl.when(jnp.logical_and(j >= 2 * ns, j < 3 * ns))
    def _():
        v_ref[...] = acc
        vb_ref[...] = acc.astype(BF16)

    @pl.when(jnp.logical_and(j >= 3 * ns, j < 4 * ns))
    def _():
        gate_ref[...] = acc.astype(BF16)

    @pl.when(j == 4 * ns)
    def _():
        lf_ref[...] = jax.nn.log_sigmoid(acc[:, 0:LANES] + bf_ref[...])


def fox_in(x, g, w, gq2, gk2, bf_pad):
    m, d = x.shape
    tm = _pick(m, (1024, 512))
    ns = FOX_W // PROJ_TN
    nj = 4 * ns + 1
    clip = lambda j, lo: jnp.clip(j - lo, 0, ns - 1)
    seg = lambda lo: pl.BlockSpec((tm, PROJ_TN), lambda i, j: (i, clip(j, lo)))
    row = pl.BlockSpec((1, LANES), lambda i, j: (0, 0))
    return pl.pallas_call(
        _fox_in_kernel,
        grid=(m // tm, nj),
        in_specs=[
            pl.BlockSpec((tm, d), lambda i, j: (i, 0)),
            pl.BlockSpec((1, d), lambda i, j: (0, 0)),
            pl.BlockSpec((d, PROJ_TN), lambda i, j: (0, j)),
            row, row, row,
        ],
        out_specs=[seg(0), seg(ns), seg(ns), seg(2 * ns), seg(2 * ns), seg(3 * ns),
                   pl.BlockSpec((tm, LANES), lambda i, j: (i, 0))],
        out_shape=[
            jax.ShapeDtypeStruct((m, FOX_W), BF16),
            jax.ShapeDtypeStruct((m, FOX_W), F32),
            jax.ShapeDtypeStruct((m, FOX_W), BF16),
            jax.ShapeDtypeStruct((m, FOX_W), F32),
            jax.ShapeDtypeStruct((m, FOX_W), BF16),
            jax.ShapeDtypeStruct((m, FOX_W), BF16),
            jax.ShapeDtypeStruct((m, LANES), F32),
        ],
        scratch_shapes=[pltpu.VMEM((tm, d), BF16)],
        compiler_params=_cparams("parallel", "arbitrary"),
        name="fox_in",
    )(x, g.reshape(1, d), w, gq2, gk2, bf_pad)


def _rope_tables(pos, half):
    inv = ROPE_THETA ** (-jnp.arange(half, dtype=F32) / half)
    ang = pos.astype(F32)[:, None] * inv[None, :]
    return jnp.cos(ang), jnp.sin(ang)


def _pad_rows(a, rows):
    pad = rows - a.shape[1]
    if pad == 0:
        return a
    return jnp.pad(a, ((0, 0), (0, pad)) + ((0, 0),) * (a.ndim - 2))


def _retention_mixer(h2, b, t, pos, s0, gmix, w_in, gn, w_out):
    cos, sin = _rope_tables(pos, RET_DK // 2)
    q, k, v, gate = ret_in(h2, gmix, w_in, cos, sin, t)
    lg = jnp.log1p(-jnp.exp2(-5.0 - jnp.arange(RET_HEADS, dtype=F32)))
    chunk = _pick(t, (256, 128, 64))
    a, s = retention(q.reshape(b, t, RET_QK), k.reshape(b, t, RET_QK), v.reshape(b, t, RET_VW),
                     gate.reshape(b, t, RET_VW), gn.reshape(1, RET_VW), s0, lg, chunk)
    return matmul_res(a.reshape(b * t, RET_VW), w_out, h2, name="ret_out"), s


def _mla_mixer(h2, b, t, pos, lat_past, kr_past, gmix, w):
    proj = norm_matmul(h2, gmix, w['w_in'], name="mla_in")
    cos, sin = _rope_tables(pos, MLA_ROPE // 2)
    z = jnp.zeros_like(cos)
    tabs = (jnp.concatenate([cos, cos, z, z], axis=1), jnp.concatenate([-sin, z, z, z], axis=1),
            jnp.concatenate([z, sin, z, z], axis=1))
    q_pad, lat, kr_pad = mla_q(proj, w['q_norm'], w['kv_norm'], w['gq_nope'], w['gq_rope_pad'], w['gk_rope_pad'],
                               w['w_qb_pad'], tabs, t)
    tk = lat_past.shape[1] + t
    tkp = -(-tk // LANES) * LANES
    if lat_past.shape[1] == 0:
        lat_all, kr_all = lat, kr_pad
    else:
        lat_all = _pad_rows(jnp.concatenate([lat_past, lat.reshape(b, t, MLA_KV_LORA)], axis=1), tkp)
        kr_past_pad = jnp.pad(kr_past, ((0, 0), (0, 0), (0, LANES - MLA_ROPE)))
        kr_all = _pad_rows(jnp.concatenate([kr_past_pad, kr_pad.reshape(b, t, LANES)], axis=1), tkp)
        lat_all = lat_all.reshape(b * tkp, MLA_KV_LORA)
        kr_all = kr_all.reshape(b * tkp, LANES)
    k_pad, v = mla_kv(lat_all, kr_all, w['gk_nope'], w['w_kb'], w['w_vb'])
    o = attention(q_pad.reshape(b, t, -1), k_pad.reshape(b, tkp, -1), v.reshape(b, tkp, -1), mode="chunk",
                  n_blocks=MLA_HEADS, hpb=1, tk_real=tk, name="mla_attn")
    h2 = matmul_res(o.reshape(b * t, MLA_HEADS * MLA_V), w['w_out'], h2, name="mla_out")
    return h2, lat.reshape(b, t, MLA_KV_LORA), kr_pad[:, :MLA_ROPE].reshape(b, t, MLA_ROPE)


def _fox_mixer(h2, b, t, k_past, v_past, lf_past, gmix, w):
    q, k, kb, v, vb, gate, lf_pad = fox_in(h2, gmix, w['w_in'], w['gq2'], w['gk2'], w['bf_pad'])
    lf = lf_pad[:, :FOX_HEADS].reshape(b, t, FOX_HEADS)
    tp = k_past.shape[1]
    tk = tp + t
    tkp = -(-tk // LANES) * LANES
    kb = kb.reshape(b, t, FOX_W)
    vb = vb.reshape(b, t, FOX_W)
    if tp > 0:
        kb = _pad_rows(jnp.concatenate([k_past.reshape(b, tp, FOX_W).astype(BF16), kb], axis=1), tkp)
        vb = _pad_rows(jnp.concatenate([v_past.reshape(b, tp, FOX_W).astype(BF16), vb], axis=1), tkp)
    lf_all = jnp.concatenate([lf_past.astype(F32), lf], axis=1)
    fcum = jnp.cumsum(lf_all, axis=1) * LOG2E
    fq = fcum[:, tk - t:].transpose(0, 2, 1)[..., None]
    fk = _pad_rows(fcum, tkp).transpose(0, 2, 1)[:, :, None, :]
    a = attention(q.reshape(b, t, FOX_W), kb, vb, mode="token", n_blocks=FOX_HEADS // 2, hpb=2, tk_real=tk,
                  name="fox_attn", fq=fq, fk=fk, gate=gate.reshape(b, t, FOX_W))
    h2 = matmul_res(a.reshape(b * t, FOX_W), w['w_out'], h2, name="fox_out")
    return (h2, k.reshape(b, t, FOX_HEADS, FOX_DH), v.reshape(b, t, FOX_HEADS, FOX_DH), lf)


def _trunk(x, p, past_len, ret_states, lat_past, kr_past, fk_past, fv_past, flf_past, w):
    b, t, d = x.shape
    pos = past_len + jnp.arange(t)
    h2 = x.reshape(b * t, d)
    new_ret, new_lat, new_kr, new_fk, new_fv, new_flf = [], [], [], [], [], []
    for i in range(DEPTH):
        kind, j = i % N_MIXERS, i // N_MIXERS
        if kind == 0:
            h2, s = _retention_mixer(h2, b, t, pos, ret_states[j], w['norm_mix'][i], w['ret_w_in'][j],
                                     w['ret_gn'][j], w['ret_w_out'][j])
            new_ret.append(s)
        elif kind == 1:
            h2, lat, kr = _mla_mixer(h2, b, t, pos, lat_past[j], kr_past[j], w['norm_mix'][i], w['mla'][j])
            new_lat.append(lat)
            new_kr.append(kr)
        else:
            h2, fk, fv, flf = _fox_mixer(h2, b, t, fk_past[j], fv_past[j], flf_past[j], w['norm_mix'][i],
                                         w['fox'][j])
            new_fk.append(fk)
            new_fv.append(fv)
            new_flf.append(flf)
        h2 = ffn(h2, w['norm_ffn'][i], w['ffn_w_gate'][i], w['ffn_w_up'][i], w['ffn_w_down'][i])
        h2 = pe_inject(h2, p[i].reshape(b * t, PE_DIM), w['norm_pe'][i], w['pe_w_gate'][i], w['pe_w_proj'][i],
                       final_gain=w['norm_final'] if i == DEPTH - 1 else None)
    return (h2.reshape(b, t, d), jnp.stack(new_ret), jnp.stack(new_lat), jnp.stack(new_kr),
            jnp.stack(new_fk), jnp.stack(new_fv), jnp.stack(new_flf))


def _pad_last(a, n):
    return jnp.pad(a, [(0, 0)] * (a.ndim - 1) + [(0, n - a.shape[-1])])


def kernel(x_prompt, x_sample, state_ret, cache_mla_latent, cache_mla_krope, cache_fox_k, cache_fox_v,
           cache_fox_logf, p_prompt, p_sample, norm_mix, norm_ffn, norm_pe, norm_final, ret_w_in, ret_gn,
           ret_w_out, mla_w_in, mla_q_norm, mla_kv_norm, mla_w_qb, mla_w_kvb, mla_gq_nope, mla_gq_rope,
           mla_gk_nope, mla_gk_rope, mla_w_out, fox_w_in, fox_b_f, fox_gq, fox_gk, fox_w_out, ffn_w_gate,
           ffn_w_up, ffn_w_down, pe_w_proj, pe_w_gate):
    bf = lambda a: a.astype(BF16)
    n_ret, n_mla, n_fox = state_ret.shape[0], cache_mla_latent.shape[0], cache_fox_k.shape[0]
    mla = []
    for j in range(n_mla):
        w_qb = mla_w_qb[j].reshape(MLA_Q_LORA, MLA_HEADS, MLA_NOPE + MLA_ROPE)
        w_kvb = mla_w_kvb[j].reshape(MLA_KV_LORA, MLA_HEADS, MLA_NOPE + MLA_V)
        mla.append(dict(
            w_in=bf(_pad_last(mla_w_in[j], MLA_IN_PAD)),
            q_norm=mla_q_norm[j], kv_norm=mla_kv_norm[j], gq_nope=mla_gq_nope[j], gk_nope=mla_gk_nope[j],
            gq_rope_pad=_pad_last(mla_gq_rope[j], LANES).reshape(1, LANES),
            gk_rope_pad=_pad_last(mla_gk_rope[j], LANES).reshape(1, LANES),
            w_qb_pad=bf(_pad_last(w_qb, MLA_DK_PAD).reshape(MLA_Q_LORA, MLA_HEADS * MLA_DK_PAD)),
            w_kb=bf(w_kvb[:, :, :MLA_NOPE].reshape(MLA_KV_LORA, MLA_HEADS * MLA_NOPE)),
            w_vb=bf(w_kvb[:, :, MLA_NOPE:].reshape(MLA_KV_LORA, MLA_HEADS * MLA_V)),
            w_out=bf(mla_w_out[j])))
    fox = []
    for j in range(n_fox):
        fox.append(dict(
            w_in=bf(_pad_last(fox_w_in[j], 4 * FOX_W + PROJ_TN)),
            gq2=jnp.tile(fox_gq[j], 2).reshape(1, LANES), gk2=jnp.tile(fox_gk[j], 2).reshape(1, LANES),
            bf_pad=_pad_last(fox_b_f[j], LANES).reshape(1, LANES),
            w_out=bf(fox_w_out[j])))
    w = dict(norm_mix=norm_mix, norm_ffn=norm_ffn, norm_pe=norm_pe, norm_final=norm_final,
             ret_w_in=bf(ret_w_in), ret_gn=ret_gn, ret_w_out=bf(ret_w_out), mla=mla, fox=fox,
             ffn_w_gate=bf(ffn_w_gate), ffn_w_up=bf(ffn_w_up), ffn_w_down=bf(ffn_w_down),
             pe_w_proj=bf(pe_w_proj), pe_w_gate=bf(pe_w_gate))
    bp = x_prompt.shape[0]
    dt = x_prompt.dtype
    y_p, ret_p, lat_p, kr_p, fk_p, fv_p, flf_p = _trunk(
        x_prompt, p_prompt, 0,
        jnp.zeros((n_ret, bp, RET_HEADS, RET_DK, RET_DV), dt),
        jnp.zeros((n_mla, bp, 0, MLA_KV_LORA), dt), jnp.zeros((n_mla, bp, 0, MLA_ROPE), dt),
        jnp.zeros((n_fox, bp, 0, FOX_HEADS, FOX_DH), dt), jnp.zeros((n_fox, bp, 0, FOX_HEADS, FOX_DH), dt),
        jnp.zeros((n_fox, bp, 0, FOX_HEADS), dt), w)
    y_s, ret_s, lat_s, kr_s, fk_s, fv_s, flf_s = _trunk(
        x_sample, p_sample, cache_mla_latent.shape[2], state_ret, cache_mla_latent, cache_mla_krope,
        cache_fox_k, cache_fox_v, cache_fox_logf, w)
    return (y_p, y_s, ret_p, ret_s, lat_p, kr_p, lat_s, kr_s, fk_p, fv_p, flf_p, fk_s, fv_s, flf_s)
```

```python
import functools
import math

import jax
import jax.numpy as jnp
from jax import lax
from jax.experimental import pallas as pl
from jax.experimental.pallas import tpu as pltpu

D_MODEL = 1024
DEPTH = 4
CHUNK = 64
PE_DIM = 256
N_MIXERS = 3
EPS = 1e-6
ROPE_THETA = 10000.0
RET_HEADS = 4
RET_DK = 256
RET_DV = 512
RET_QK = RET_HEADS * RET_DK
RET_VW = RET_HEADS * RET_DV
MLA_HEADS = 16
MLA_NOPE = 128
MLA_ROPE = 64
MLA_V = 128
MLA_Q_LORA = 512
MLA_KV_LORA = 256
FOX_HEADS = 16
FOX_DH = 64
FOX_W = FOX_HEADS * FOX_DH
D_FF = ((8 * D_MODEL + 3 * 256 - 1) // (3 * 256)) * 256

LANES = 128
MLA_DK_PAD = 2 * LANES
MLA_IN_PAD = 7 * LANES
PROJ_TN = 512
FOX_SEG = FOX_W + LANES
MASK_VALUE = -1e30
LOG2E = math.log2(math.e)
VMEM_LIMIT = 48 * 1024 * 1024
VMEM_LIMIT_BIG = 56 * 1024 * 1024

F32 = jnp.float32
BF16 = jnp.bfloat16
NT_DIMS = (((1,), (1,)), ((), ()))


def _cparams(*sem, vmem=VMEM_LIMIT):
    return pltpu.CompilerParams(dimension_semantics=sem, vmem_limit_bytes=vmem)


def _pick(n, prefs):
    for p in prefs:
        if n % p == 0:
            return p
    return n


def _rms_bf16(x, g):
    ms = jnp.mean(x * x, axis=-1, keepdims=True)
    return (x * lax.rsqrt(ms + EPS) * g).astype(BF16)


def _pos_table_spec(t, tm):
    if t % tm == 0:
        nt = t // tm
        return pl.BlockSpec((tm, LANES), lambda i, *_: (i % nt, 0))
    return pl.BlockSpec((tm, LANES), lambda i, *_: (0, 0))


def _pos_table(tab, t, tm):
    return tab if t % tm == 0 else jnp.tile(tab, (tm // t, 1))


def _norm_mm_kernel(x_ref, g_ref, w_ref, o_ref, xn_ref):
    @pl.when(pl.program_id(1) == 0)
    def _():
        xn_ref[...] = _rms_bf16(x_ref[...], g_ref[...])

    o_ref[...] = jnp.dot(xn_ref[...], w_ref[...], preferred_element_type=F32).astype(o_ref.dtype)


def norm_matmul(x, g, w, out_dtype=F32, name="norm_mm"):
    m, k = x.shape
    n = w.shape[1]
    tm = _pick(m, (1024, 512, 256))
    tn = _pick(n, (1024, 768, 512, 896, 128))
    return pl.pallas_call(
        _norm_mm_kernel,
        grid=(m // tm, n // tn),
        in_specs=[
            pl.BlockSpec((tm, k), lambda i, j: (i, 0)),
            pl.BlockSpec((1, k), lambda i, j: (0, 0)),
            pl.BlockSpec((k, tn), lambda i, j: (0, j)),
        ],
        out_specs=pl.BlockSpec((tm, tn), lambda i, j: (i, j)),
        out_shape=jax.ShapeDtypeStruct((m, n), out_dtype),
        scratch_shapes=[pltpu.VMEM((tm, k), BF16)],
        compiler_params=_cparams("parallel", "arbitrary"),
        name=name,
    )(x, g.reshape(1, k), w)


def _mm_res_kernel(a_ref, w_ref, r_ref, o_ref):
    o_ref[...] = r_ref[...] + jnp.dot(a_ref[...], w_ref[...], preferred_element_type=F32)


def matmul_res(a, w, res, name="mm"):
    m, k = a.shape
    n = w.shape[1]
    tm = _pick(m, (1024, 512, 256))
    tn = _pick(n, (1024, 512, 128))
    return pl.pallas_call(
        _mm_res_kernel,
        grid=(m // tm, n // tn),
        in_specs=[
            pl.BlockSpec((tm, k), lambda i, j: (i, 0)),
            pl.BlockSpec((k, tn), lambda i, j: (0, j)),
            pl.BlockSpec((tm, tn), lambda i, j: (i, j)),
        ],
        out_specs=pl.BlockSpec((tm, tn), lambda i, j: (i, j)),
        out_shape=jax.ShapeDtypeStruct((m, n), F32),
        compiler_params=_cparams("parallel", "arbitrary"),
        name=name,
    )(a, w, res)


def _ffn_kernel(x_ref, g_ref, wg_ref, wu_ref, wd_ref, o_ref, xn_ref):
    @pl.when(pl.program_id(1) == 0)
    def _():
        x = x_ref[...]
        xn_ref[...] = _rms_bf16(x, g_ref[...])
        o_ref[...] = x

    xn = xn_ref[...]
    tf = wg_ref.shape[1]
    for lo in range(0, tf, PROJ_TN):
        wd_rows = min(PROJ_TN, tf - lo)
        gate = jnp.dot(xn, wg_ref[:, lo:lo + wd_rows], preferred_element_type=F32)
        up = jnp.dot(xn, wu_ref[:, lo:lo + wd_rows], preferred_element_type=F32)
        act = (gate * jax.nn.sigmoid(gate) * up).astype(BF16)
        o_ref[...] += jnp.dot(act, wd_ref[lo:lo + wd_rows, :], preferred_element_type=F32)


def ffn(h, g, wg, wu, wd):
    m, d = h.shape
    dff = wg.shape[1]
    tm = _pick(m, (1024, 512))
    tf = dff // 2
    return pl.pallas_call(
        _ffn_kernel,
        grid=(m // tm, dff // tf),
        in_specs=[
            pl.BlockSpec((tm, d), lambda i, f: (i, 0)),
            pl.BlockSpec((1, d), lambda i, f: (0, 0)),
            pl.BlockSpec((d, tf), lambda i, f: (0, f)),
            pl.BlockSpec((d, tf), lambda i, f: (0, f)),
            pl.BlockSpec((tf, d), lambda i, f: (f, 0)),
        ],
        out_specs=pl.BlockSpec((tm, d), lambda i, f: (i, 0)),
        out_shape=jax.ShapeDtypeStruct((m, d), F32),
        scratch_shapes=[pltpu.VMEM((tm, d), BF16)],
        compiler_params=_cparams("parallel", "arbitrary", vmem=VMEM_LIMIT_BIG),
        name="ffn",
    )(h, g.reshape(1, d), wg, wu, wd)


def _pe_kernel(h_ref, p_ref, g_ref, wg_ref, wp_ref, *rest, final):
    o_ref = rest[-1]
    h = h_ref[...]
    hn = _rms_bf16(h, g_ref[...])
    gate = jax.nn.sigmoid(jnp.dot(hn, wg_ref[...], preferred_element_type=F32))
    proj = jnp.dot(p_ref[...].astype(BF16), wp_ref[...], preferred_element_type=F32)
    out = h + gate * proj
    if final:
        ms2 = jnp.mean(out * out, axis=-1, keepdims=True)
        out = out * lax.rsqrt(ms2 + EPS) * rest[0][...]
    o_ref[...] = out


def pe_inject(h, p, layer, g, wg, wp, final_gain=None):
    m, d = h.shape
    pd = p.shape[2]
    tm = _pick(m, (512, 256))
    in_specs = [
        pl.BlockSpec((tm, d), lambda i: (i, 0)),
        pl.BlockSpec((None, tm, pd), lambda i: (layer, i, 0)),
        pl.BlockSpec((1, d), lambda i: (0, 0)),
        pl.BlockSpec((d, d), lambda i: (0, 0)),
        pl.BlockSpec((pd, d), lambda i: (0, 0)),
    ]
    args = [h, p, g.reshape(1, d), wg, wp]
    if final_gain is not None:
        in_specs.append(pl.BlockSpec((1, d), lambda i: (0, 0)))
        args.append(final_gain.reshape(1, d))
    return pl.pallas_call(
        functools.partial(_pe_kernel, final=final_gain is not None),
        grid=(m // tm,),
        in_specs=in_specs,
        out_specs=pl.BlockSpec((tm, d), lambda i: (i, 0)),
        out_shape=jax.ShapeDtypeStruct((m, d), F32),
        compiler_params=_cparams("parallel"),
        name="pe_inject",
    )(*args)


def _attn_kernel(*refs, tq, nq, tkp, mw, hpb, has_bias, has_gate):
    q_ref, k_ref, v_ref, mb_ref = refs[:4]
    pos = 4
    if has_bias:
        fq_ref, fk_ref = refs[4:6]
        pos = 6
    if has_gate:
        g_ref = refs[pos]
        pos += 1
    o_ref = refs[pos]
    bw_k = q_ref.shape[2]
    bw_v = v_ref.shape[2]
    mb = mb_ref[...]
    lane_k = lax.broadcasted_iota(jnp.int32, (tq, bw_k), 1)
    lane_v = lax.broadcasted_iota(jnp.int32, (tq, bw_v), 1)

    for qi in range(nq):
        r0 = qi * tq
        kv_len = tkp - (nq - 1 - qi) * tq
        head_len = kv_len - mw
        q_blk = q_ref[0, r0:r0 + tq, :]
        out = None
        for hh in range(hpb):
            q = q_blk if hpb == 1 else jnp.where(lane_k // (bw_k // hpb) == hh, q_blk, jnp.zeros_like(q_blk))

            def scores(lo, hi):
                s = lax.dot_general(q, k_ref[0, lo:hi, :], NT_DIMS, preferred_element_type=F32)
                if has_bias:
                    s = s - fk_ref[0, hh, :, lo:hi]
                return s

            s_tail = scores(head_len, kv_len) + mb
            m = jnp.max(s_tail, axis=1, keepdims=True)
            if head_len > 0:
                s_head = scores(0, head_len)
                m = jnp.maximum(m, jnp.max(s_head, axis=1, keepdims=True))
            if has_bias:
                head = pl.program_id(1) * hpb + hh
                f_rows = fq_ref[0, r0:r0 + tq, :]
                lane_f = lax.broadcasted_iota(jnp.int32, f_rows.shape, 1)
                fq = jnp.sum(jnp.where(lane_f == head, f_rows, 0.0), axis=1, keepdims=True)
                shift = fq - (fq + m)
            else:
                shift = -m
            p_tail = jnp.exp2(s_tail + shift)
            l = jnp.sum(p_tail, axis=1, keepdims=True)
            pv = jnp.dot(p_tail.astype(BF16), v_ref[0, head_len:kv_len, :], preferred_element_type=F32)
            if head_len > 0:
                p_head = jnp.exp2(s_head + shift)
                l = l + jnp.sum(p_head, axis=1, keepdims=True)
                pv = pv + jnp.dot(p_head.astype(BF16), v_ref[0, 0:head_len, :], preferred_element_type=F32)
            o_h = pv * (1.0 / l)
            out = o_h if hh == 0 else jnp.where(lane_v // (bw_v // hpb) == hh, o_h, out)
        if has_gate:
            out = jax.nn.sigmoid(g_ref[0, r0:r0 + tq, :].astype(F32)) * out
        o_ref[0, r0:r0 + tq, :] = out.astype(o_ref.dtype)


def attention(q, k, v, *, mode, n_blocks, hpb, tk_real, name, fq=None, fk=None, gate=None):
    b, tq_all, wq = q.shape
    tkp = k.shape[1]
    bw_k = wq // n_blocks
    bw_v = v.shape[2] // n_blocks
    tq = _pick(tq_all, (512,))
    nq = tq_all // tq
    mw = tq if nq > 1 or tq_all == tkp else LANES
    q0 = tk_real - tq
    k0 = tkp - mw
    qpos = q0 + lax.broadcasted_iota(jnp.int32, (tq, mw), 0)
    kpos = k0 + lax.broadcasted_iota(jnp.int32, (tq, mw), 1)
    allowed = (kpos // CHUNK <= qpos // CHUNK) if mode == "chunk" else (kpos <= qpos)
    allowed = jnp.logical_and(allowed, kpos < tk_real)
    mb = jnp.where(allowed, 0.0, MASK_VALUE).astype(F32)

    in_specs = [
        pl.BlockSpec((1, tq_all, bw_k), lambda bi, hi: (bi, 0, hi)),
        pl.BlockSpec((1, tkp, bw_k), lambda bi, hi: (bi, 0, hi)),
        pl.BlockSpec((1, tkp, bw_v), lambda bi, hi: (bi, 0, hi)),
        pl.BlockSpec((tq, mw), lambda bi, hi: (0, 0)),
    ]
    args = [q, k, v, mb]
    if fq is not None:
        in_specs += [
            pl.BlockSpec((1, tq_all, LANES), lambda bi, hi: (bi, 0, 0)),
            pl.BlockSpec((1, hpb, 1, tkp), lambda bi, hi: (bi, hi, 0, 0)),
        ]
        args += [fq, fk]
    if gate is not None:
        in_specs.append(pl.BlockSpec((1, tq_all, bw_v), lambda bi, hi: (bi, 0, hi)))
        args.append(gate)
    kern = functools.partial(_attn_kernel, tq=tq, nq=nq, tkp=tkp, mw=mw, hpb=hpb,
                             has_bias=fq is not None, has_gate=gate is not None)
    return pl.pallas_call(
        kern,
        grid=(b, n_blocks),
        in_specs=in_specs,
        out_specs=pl.BlockSpec((1, tq_all, bw_v), lambda bi, hi: (bi, 0, hi)),
        out_shape=jax.ShapeDtypeStruct((b, tq_all, n_blocks * bw_v), BF16),
        compiler_params=_cparams("parallel", "parallel"),
        name=name,
    )(*args)


def _ret_in_kernel(x_ref, g_ref, w_ref, cos_ref, sin_ref, q_ref, k_ref, v_ref, gate_ref, xn_ref):
    j = pl.program_id(1)
    nv = RET_VW // RET_QK

    @pl.when(j == 0)
    def _():
        xn_ref[...] = _rms_bf16(x_ref[...], g_ref[...])

    def chunks():
        for lo in range(0, RET_QK, PROJ_TN):
            yield lo, jnp.dot(xn_ref[...], w_ref[:, lo:lo + PROJ_TN], preferred_element_type=F32)

    def rope_store(dst, mult):
        cos = cos_ref[...]
        sin = sin_ref[...]
        half = RET_DK // 2
        for lo, acc in chunks():
            for hh in range(PROJ_TN // RET_DK):
                c0 = hh * RET_DK
                x1 = acc[:, c0:c0 + half]
                x2 = acc[:, c0 + half:c0 + RET_DK]
                dst[:, lo + c0:lo + c0 + half] = ((x1 * cos - x2 * sin) * mult).astype(BF16)
                dst[:, lo + c0 + half:lo + c0 + RET_DK] = ((x2 * cos + x1 * sin) * mult).astype(BF16)

    def plain_store(dst):
        for lo, acc in chunks():
            dst[:, lo:lo + PROJ_TN] = acc.astype(BF16)

    @pl.when(j == 0)
    def _():
        rope_store(q_ref, 1.0)

    @pl.when(j == 1)
    def _():
        rope_store(k_ref, RET_DK ** -0.5)

    @pl.when(jnp.logical_and(j >= 2, j < 2 + nv))
    def _():
        plain_store(v_ref)

    @pl.when(j >= 2 + nv)
    def _():
        plain_store(gate_ref)


def ret_in(x, g, w, cos, sin, t):
    m, d = x.shape
    tm = _pick(m, (1024, 512))
    nv = RET_VW // RET_QK
    nj = 2 + 2 * nv
    clip = lambda j, lo, n: jnp.clip(j - lo, 0, n - 1)
    tab = _pos_table_spec(t, tm)
    return pl.pallas_call(
        _ret_in_kernel,
        grid=(m // tm, nj),
        in_specs=[
            pl.BlockSpec((tm, d), lambda i, j: (i, 0)),
            pl.BlockSpec((1, d), lambda i, j: (0, 0)),
            pl.BlockSpec((d, RET_QK), lambda i, j: (0, j)),
            tab, tab,
        ],
        out_specs=[
            pl.BlockSpec((tm, RET_QK), lambda i, j: (i, 0)),
            pl.BlockSpec((tm, RET_QK), lambda i, j: (i, 0)),
            pl.BlockSpec((tm, RET_QK), lambda i, j: (i, clip(j, 2, nv))),
            pl.BlockSpec((tm, RET_QK), lambda i, j: (i, clip(j, 2 + nv, nv))),
        ],
        out_shape=[
            jax.ShapeDtypeStruct((m, RET_QK), BF16),
            jax.ShapeDtypeStruct((m, RET_QK), BF16),
            jax.ShapeDtypeStruct((m, RET_VW), BF16),
            jax.ShapeDtypeStruct((m, RET_VW), BF16),
        ],
        scratch_shapes=[pltpu.VMEM((tm, d), BF16)],
        compiler_params=_cparams("parallel", "arbitrary", vmem=VMEM_LIMIT_BIG),
        name="ret_in",
    )(x, g.reshape(1, d), w, _pos_table(cos, t, tm), _pos_table(sin, t, tm))


def _retention_kernel(lg_ref, q_ref, k_ref, v_ref, gate_ref, gn_ref, s0_ref, a_ref, s_ref, *, chunk, nc):
    lg = lg_ref[pl.program_id(1)]
    row = lax.broadcasted_iota(jnp.int32, (chunk, chunk), 0)
    col = lax.broadcasted_iota(jnp.int32, (chunk, chunk), 1)
    diff = row - col
    dmask = jnp.where(diff >= 0, jnp.exp(lg * jnp.maximum(diff, 0).astype(F32)), 0.0)
    idx = lax.broadcasted_iota(jnp.int32, (chunk, 1), 0).astype(F32)
    q_dec = jnp.exp(lg * (idx + 1.0))
    k_dec = jnp.exp(lg * (chunk - 1.0 - idx))
    c_dec = jnp.exp(lg * jnp.full((1, RET_DV), float(chunk), F32))
    gn = gn_ref[...]

    state = s0_ref[0, 0]
    for c in range(nc):
        rows = slice(c * chunk, (c + 1) * chunk)
        q = q_ref[0, rows, :]
        k = k_ref[0, rows, :]
        v = v_ref[0, rows, :]
        att = lax.dot_general(q, k, NT_DIMS, preferred_element_type=F32) * dmask
        qd = (q.astype(F32) * q_dec).astype(BF16)
        o = jnp.dot(att.astype(BF16), v, preferred_element_type=F32)
        o = o + jnp.dot(qd, state.astype(BF16), preferred_element_type=F32)
        kd = (k.astype(F32) * k_dec).astype(BF16)
        upd = lax.dot_general(kd, v, (((0,), (0,)), ((), ())), preferred_element_type=F32)
        state = state * c_dec + upd

        mu = jnp.mean(o, axis=-1, keepdims=True)
        oc = o - mu
        var = jnp.mean(oc * oc, axis=-1, keepdims=True)
        y = oc * lax.rsqrt(var + EPS) * gn
        g = gate_ref[0, rows, :].astype(F32)
        a_ref[0, rows, :] = (g * jax.nn.sigmoid(g) * y).astype(BF16)
    s_ref[0, 0] = state


def retention(q, k, v, gate, gn, s0, lg, chunk):
    b, t, _ = q.shape
    nc = t // chunk
    kern = functools.partial(_retention_kernel, chunk=chunk, nc=nc)
    tok = lambda bi, hi, lg: (bi, 0, hi)
    st = lambda bi, hi, lg: (bi, hi, 0, 0)
    return pl.pallas_call(
        kern,
        grid_spec=pltpu.PrefetchScalarGridSpec(
            num_scalar_prefetch=1,
            grid=(b, RET_HEADS),
            in_specs=[
                pl.BlockSpec((1, t, RET_DK), tok),
                pl.BlockSpec((1, t, RET_DK), tok),
                pl.BlockSpec((1, t, RET_DV), tok),
                pl.BlockSpec((1, t, RET_DV), tok),
                pl.BlockSpec((1, RET_DV), lambda bi, hi, lg: (0, hi)),
                pl.BlockSpec((1, 1, RET_DK, RET_DV), st),
            ],
            out_specs=[
                pl.BlockSpec((1, t, RET_DV), tok),
                pl.BlockSpec((1, 1, RET_DK, RET_DV), st),
            ],
        ),
        out_shape=[
            jax.ShapeDtypeStruct((b, t, RET_VW), BF16),
            jax.ShapeDtypeStruct((b, RET_HEADS, RET_DK, RET_DV), F32),
        ],
        compiler_params=_cparams("parallel", "parallel"),
        name="retention",
    )(lg, q, k, v, gate, gn, s0)


def _rope_lanes(x, c_tab, s1_tab, s2_tab):
    return x * c_tab + pltpu.roll(x, 3 * LANES // 4, 1) * s1_tab + pltpu.roll(x, LANES // 4, 1) * s2_tab


def _mla_q_kernel(p_ref, qn_ref, kvn_ref, gqn_ref, gqr_ref, gkr_ref, w_ref, c_ref, s1_ref, s2_ref,
                  q_ref, lat_ref, kr_ref):
    c_tab, s1_tab, s2_tab = c_ref[...], s1_ref[...], s2_ref[...]
    qscale = (MLA_NOPE + MLA_ROPE) ** -0.5 * LOG2E

    ckv = p_ref[:, MLA_Q_LORA:MLA_Q_LORA + MLA_KV_LORA]
    lat_ref[...] = ckv * lax.rsqrt(jnp.mean(ckv * ckv, axis=-1, keepdims=True) + EPS) * kvn_ref[...]

    krp = p_ref[:, MLA_Q_LORA + MLA_KV_LORA:MLA_IN_PAD]
    krn = krp * lax.rsqrt(jnp.sum(krp * krp, axis=-1, keepdims=True) * (1.0 / MLA_ROPE) + EPS) * gkr_ref[...]
    kr_ref[...] = _rope_lanes(krn, c_tab, s1_tab, s2_tab)

    cqn = _rms_bf16(p_ref[:, 0:MLA_Q_LORA], qn_ref[...])
    hg = 4
    for g0 in range(0, MLA_HEADS, hg):
        acc = jnp.dot(cqn, w_ref[:, g0 * MLA_DK_PAD:(g0 + hg) * MLA_DK_PAD], preferred_element_type=F32)
        for hh in range(hg):
            a = acc[:, hh * MLA_DK_PAD:hh * MLA_DK_PAD + LANES]
            r = acc[:, hh * MLA_DK_PAD + LANES:(hh + 1) * MLA_DK_PAD]
            an = a * lax.rsqrt(jnp.mean(a * a, axis=-1, keepdims=True) + EPS) * gqn_ref[...]
            rn = r * lax.rsqrt(jnp.sum(r * r, axis=-1, keepdims=True) * (1.0 / MLA_ROPE) + EPS) * gqr_ref[...]
            col = (g0 + hh) * MLA_DK_PAD
            q_ref[:, col:col + LANES] = (an * qscale).astype(BF16)
            q_ref[:, col + LANES:col + MLA_DK_PAD] = (_rope_lanes(rn, c_tab, s1_tab, s2_tab) * qscale).astype(BF16)


def mla_q(proj, q_norm, kv_norm, gq_nope, gq_rope_pad, gk_rope_pad, w_qb_pad, tabs, t):
    m = proj.shape[0]
    tm = _pick(m, (512, 256))
    row = lambda n: pl.BlockSpec((1, n), lambda i: (0, 0))
    tab = _pos_table_spec(t, tm)
    return pl.pallas_call(
        _mla_q_kernel,
        grid=(m // tm,),
        in_specs=[
            pl.BlockSpec((tm, MLA_IN_PAD), lambda i: (i, 0)),
            row(MLA_Q_LORA), row(MLA_KV_LORA), row(LANES), row(LANES), row(LANES),
            pl.BlockSpec((MLA_Q_LORA, MLA_HEADS * MLA_DK_PAD), lambda i: (0, 0)),
            tab, tab, tab,
        ],
        out_specs=[
            pl.BlockSpec((tm, MLA_HEADS * MLA_DK_PAD), lambda i: (i, 0)),
            pl.BlockSpec((tm, MLA_KV_LORA), lambda i: (i, 0)),
            pl.BlockSpec((tm, LANES), lambda i: (i, 0)),
        ],
        out_shape=[
            jax.ShapeDtypeStruct((m, MLA_HEADS * MLA_DK_PAD), BF16),
            jax.ShapeDtypeStruct((m, MLA_KV_LORA), F32),
            jax.ShapeDtypeStruct((m, LANES), F32),
        ],
        compiler_params=_cparams("parallel"),
        name="mla_q",
    )(proj, q_norm.reshape(1, -1), kv_norm.reshape(1, -1), gq_nope.reshape(1, -1), gq_rope_pad, gk_rope_pad,
      w_qb_pad, *[_pos_table(x, t, tm) for x in tabs])


def _mla_kv_kernel(lat_ref, kr_ref, gk_ref, wk_ref, wv_ref, k_ref, v_ref):
    latb = lat_ref[...].astype(BF16)
    krb = kr_ref[...].astype(BF16)
    v_ref[...] = jnp.dot(latb, wv_ref[...], preferred_element_type=F32).astype(BF16)
    hg = 4
    for g0 in range(0, MLA_HEADS, hg):
        acc = jnp.dot(latb, wk_ref[:, g0 * MLA_NOPE:(g0 + hg) * MLA_NOPE], preferred_element_type=F32)
        for hh in range(hg):
            a = acc[:, hh * MLA_NOPE:(hh + 1) * MLA_NOPE]
            an = a * lax.rsqrt(jnp.mean(a * a, axis=-1, keepdims=True) + EPS) * gk_ref[...]
            col = (g0 + hh) * MLA_DK_PAD
            k_ref[:, col:col + LANES] = an.astype(BF16)
            k_ref[:, col + LANES:col + MLA_DK_PAD] = krb


def mla_kv(lat_all, kr_all, gk_nope, wk, wv):
    m = lat_all.shape[0]
    tm = _pick(m, (512, 256))
    return pl.pallas_call(
        _mla_kv_kernel,
        grid=(m // tm,),
        in_specs=[
            pl.BlockSpec((tm, MLA_KV_LORA), lambda i: (i, 0)),
            pl.BlockSpec((tm, LANES), lambda i: (i, 0)),
            pl.BlockSpec((1, MLA_NOPE), lambda i: (0, 0)),
            pl.BlockSpec((MLA_KV_LORA, MLA_HEADS * MLA_NOPE), lambda i: (0, 0)),
            pl.BlockSpec((MLA_KV_LORA, MLA_HEADS * MLA_V), lambda i: (0, 0)),
        ],
        out_specs=[
            pl.BlockSpec((tm, MLA_HEADS * MLA_DK_PAD), lambda i: (i, 0)),
            pl.BlockSpec((tm, MLA_HEADS * MLA_V), lambda i: (i, 0)),
        ],
        out_shape=[
            jax.ShapeDtypeStruct((m, MLA_HEADS * MLA_DK_PAD), BF16),
            jax.ShapeDtypeStruct((m, MLA_HEADS * MLA_V), BF16),
        ],
        compiler_params=_cparams("parallel"),
        name="mla_kv",
    )(lat_all, kr_all, gk_nope.reshape(1, -1), wk, wv)


def _head_norm(x, gain):
    lane = lax.broadcasted_iota(jnp.int32, (x.shape[0], LANES), 1)
    low = lane < FOX_DH
    outs = []
    for c in range(x.shape[1] // LANES):
        xb = x[:, c * LANES:(c + 1) * LANES]
        sq = xb * xb
        s_lo = jnp.sum(jnp.where(low, sq, 0.0), axis=-1, keepdims=True)
        s_hi = jnp.sum(jnp.where(low, 0.0, sq), axis=-1, keepdims=True)
        ms = jnp.where(low, s_lo, s_hi) * (1.0 / FOX_DH)
        outs.append(xb * lax.rsqrt(ms + EPS) * gain)
    return outs


def _fox_in_kernel(x_ref, g_ref, w_ref, gq_ref, gk_ref, bf_ref,
                   q_ref, k_ref, kb_ref, v_ref, vb_ref, gate_ref, lf_ref, xn_ref):
    j = pl.program_id(1)

    @pl.when(j == 0)
    def _():
        xn_ref[...] = _rms_bf16(x_ref[...], g_ref[...])

    def chunks():
        for lo in range(0, FOX_W, PROJ_TN):
            yield lo, jnp.dot(xn_ref[...], w_ref[:, lo:lo + PROJ_TN], preferred_element_type=F32)

    @pl.when(j == 0)
    def _():
        for lo, acc in chunks():
            for c, blk in enumerate(_head_norm(acc, gq_ref[...])):
                q_ref[:, lo + c * LANES:lo + (c + 1) * LANES] = (blk * (FOX_DH ** -0.5 * LOG2E)).astype(BF16)
        fl = jnp.dot(xn_ref[...], w_ref[:, FOX_W:FOX_SEG], preferred_element_type=F32)
        lf_ref[...] = jax.nn.log_sigmoid(fl + bf_ref[...])

    @pl.when(j == 1)
    def _():
        for lo, acc in chunks():
            for c, blk in enumerate(_head_norm(acc, gk_ref[...])):
                k_ref[:, lo + c * LANES:lo + (c + 1) * LANES] = blk
                kb_ref[:, lo + c * LANES:lo + (c + 1) * LANES] = blk.astype(BF16)

    @pl.when(j == 2)
    def _():
        for lo, acc in chunks():
            v_ref[:, lo:lo + PROJ_TN] = acc
            vb_ref[:, lo:lo + PROJ_TN] = acc.astype(BF16)

    @pl.when(j == 3)
    def _():
        for lo, acc in chunks():
            gate_ref[:, lo:lo + PROJ_TN] = acc.astype(BF16)


def fox_in(x, g, w, gq2, gk2, bf_pad):
    m, d = x.shape
    tm = _pick(m, (512, 256))
    seg = lambda: pl.BlockSpec((tm, FOX_W), lambda i, j: (i, 0))
    row = pl.BlockSpec((1, LANES), lambda i, j: (0, 0))
    return pl.pallas_call(
        _fox_in_kernel,
        grid=(m // tm, 4),
        in_specs=[
            pl.BlockSpec((tm, d), lambda i, j: (i, 0)),
            pl.BlockSpec((1, d), lambda i, j: (0, 0)),
            pl.BlockSpec((d, FOX_SEG), lambda i, j: (0, j)),
            row, row, row,
        ],
        out_specs=[seg(), seg(), seg(), seg(), seg(), seg(),
                   pl.BlockSpec((tm, LANES), lambda i, j: (i, 0))],
        out_shape=[
            jax.ShapeDtypeStruct((m, FOX_W), BF16),
            jax.ShapeDtypeStruct((m, FOX_W), F32),
            jax.ShapeDtypeStruct((m, FOX_W), BF16),
            jax.ShapeDtypeStruct((m, FOX_W), F32),
            jax.ShapeDtypeStruct((m, FOX_W), BF16),
            jax.ShapeDtypeStruct((m, FOX_W), BF16),
            jax.ShapeDtypeStruct((m, LANES), F32),
        ],
        scratch_shapes=[pltpu.VMEM((tm, d), BF16)],
        compiler_params=_cparams("parallel", "arbitrary"),
        name="fox_in",
    )(x, g.reshape(1, d), w, gq2, gk2, bf_pad)


def _rope_tables(pos, half):
    inv = ROPE_THETA ** (-jnp.arange(half, dtype=F32) / half)
    ang = pos.astype(F32)[:, None] * inv[None, :]
    return jnp.cos(ang), jnp.sin(ang)


def _pad_rows(a, rows):
    pad = rows - a.shape[1]
    if pad == 0:
        return a
    return jnp.pad(a, ((0, 0), (0, pad)) + ((0, 0),) * (a.ndim - 2))


def _retention_mixer(h2, b, t, pos, s0, gmix, w_in, gn, w_out):
    cos, sin = _rope_tables(pos, RET_DK // 2)
    q, k, v, gate = ret_in(h2, gmix, w_in, cos, sin, t)
    lg = jnp.log1p(-jnp.exp2(-5.0 - jnp.arange(RET_HEADS, dtype=F32)))
    chunk = _pick(t, (256, 128, 64))
    a, s = retention(q.reshape(b, t, RET_QK), k.reshape(b, t, RET_QK), v.reshape(b, t, RET_VW),
                     gate.reshape(b, t, RET_VW), gn.reshape(1, RET_VW), s0, lg, chunk)
    return matmul_res(a.reshape(b * t, RET_VW), w_out, h2, name="ret_out"), s


def _mla_mixer(h2, b, t, pos, lat_past, kr_past, gmix, w):
    proj = norm_matmul(h2, gmix, w['w_in'], name="mla_in")
    cos, sin = _rope_tables(pos, MLA_ROPE // 2)
    z = jnp.zeros_like(cos)
    tabs = (jnp.concatenate([cos, cos, z, z], axis=1), jnp.concatenate([-sin, z, z, z], axis=1),
            jnp.concatenate([z, sin, z, z], axis=1))
    q_pad, lat, kr_pad = mla_q(proj, w['q_norm'], w['kv_norm'], w['gq_nope'], w['gq_rope_pad'], w['gk_rope_pad'],
                               w['w_qb_pad'], tabs, t)
    tk = lat_past.shape[1] + t
    tkp = -(-tk // LANES) * LANES
    if lat_past.shape[1] == 0:
        lat_all, kr_all = lat, kr_pad
    else:
        lat_all = _pad_rows(jnp.concatenate([lat_past, lat.reshape(b, t, MLA_KV_LORA)], axis=1), tkp)
        kr_past_pad = jnp.pad(kr_past, ((0, 0), (0, 0), (0, LANES - MLA_ROPE)))
        kr_all = _pad_rows(jnp.concatenate([kr_past_pad, kr_pad.reshape(b, t, LANES)], axis=1), tkp)
        lat_all = lat_all.reshape(b * tkp, MLA_KV_LORA)
        kr_all = kr_all.reshape(b * tkp, LANES)
    k_pad, v = mla_kv(lat_all, kr_all, w['gk_nope'], w['w_kb'], w['w_vb'])
    o = attention(q_pad.reshape(b, t, -1), k_pad.reshape(b, tkp, -1), v.reshape(b, tkp, -1), mode="chunk",
                  n_blocks=MLA_HEADS, hpb=1, tk_real=tk, name="mla_attn")
    h2 = matmul_res(o.reshape(b * t, MLA_HEADS * MLA_V), w['w_out'], h2, name="mla_out")
    return h2, lat.reshape(b, t, MLA_KV_LORA), kr_pad[:, :MLA_ROPE].reshape(b, t, MLA_ROPE)


def _fox_mixer(h2, b, t, k_past, v_past, lf_past, gmix, w):
    q, k, kb, v, vb, gate, lf_pad = fox_in(h2, gmix, w['w_in'], w['gq2'], w['gk2'], w['bf_pad'])
    lf = lf_pad[:, :FOX_HEADS].reshape(b, t, FOX_HEADS)
    tp = k_past.shape[1]
    tk = tp + t
    tkp = -(-tk // LANES) * LANES
    kb = kb.reshape(b, t, FOX_W)
    vb = vb.reshape(b, t, FOX_W)
    if tp > 0:
        kb = _pad_rows(jnp.concatenate([k_past.reshape(b, tp, FOX_W).astype(BF16), kb], axis=1), tkp)
        vb = _pad_rows(jnp.concatenate([v_past.reshape(b, tp, FOX_W).astype(BF16), vb], axis=1), tkp)
    lf_all = jnp.concatenate([lf_past.astype(F32), lf], axis=1)
    fcum = jnp.cumsum(lf_all, axis=1) * LOG2E
    fq = _pad_last(fcum[:, tk - t:], LANES)
    fk = _pad_rows(fcum, tkp).transpose(0, 2, 1)[:, :, None, :]
    a = attention(q.reshape(b, t, FOX_W), kb, vb, mode="token", n_blocks=FOX_HEADS // 2, hpb=2, tk_real=tk,
                  name="fox_attn", fq=fq, fk=fk, gate=gate.reshape(b, t, FOX_W))
    h2 = matmul_res(a.reshape(b * t, FOX_W), w['w_out'], h2, name="fox_out")
    return (h2, k.reshape(b, t, FOX_HEADS, FOX_DH), v.reshape(b, t, FOX_HEADS, FOX_DH), lf)


def _trunk(x, p, past_len, ret_states, lat_past, kr_past, fk_past, fv_past, flf_past, w):
    b, t, d = x.shape
    pos = past_len + jnp.arange(t)
    h2 = x.reshape(b * t, d)
    new_ret, new_lat, new_kr, new_fk, new_fv, new_flf = [], [], [], [], [], []
    for i in range(DEPTH):
        kind, j = i % N_MIXERS, i // N_MIXERS
        if kind == 0:
            h2, s = _retention_mixer(h2, b, t, pos, ret_states[j], w['norm_mix'][i], w['ret_w_in'][j],
                                     w['ret_gn'][j], w['ret_w_out'][j])
            new_ret.append(s)
        elif kind == 1:
            h2, lat, kr = _mla_mixer(h2, b, t, pos, lat_past[j], kr_past[j], w['norm_mix'][i], w['mla'][j])
            new_lat.append(lat)
            new_kr.append(kr)
        else:
            h2, fk, fv, flf = _fox_mixer(h2, b, t, fk_past[j], fv_past[j], flf_past[j], w['norm_mix'][i],
                                         w['fox'][j])
            new_fk.append(fk)
            new_fv.append(fv)
            new_flf.append(flf)
        h2 = ffn(h2, w['norm_ffn'][i], w['ffn_w_gate'][i], w['ffn_w_up'][i], w['ffn_w_down'][i])
        h2 = pe_inject(h2, p.reshape(DEPTH, b * t, PE_DIM), i, w['norm_pe'][i], w['pe_w_gate'][i],
                       w['pe_w_proj'][i], final_gain=w['norm_final'] if i == DEPTH - 1 else None)
    return (h2.reshape(b, t, d), jnp.stack(new_ret), jnp.stack(new_lat), jnp.stack(new_kr),
            jnp.stack(new_fk), jnp.stack(new_fv), jnp.stack(new_flf))


def _pad_last(a, n):
    return jnp.pad(a, [(0, 0)] * (a.ndim - 1) + [(0, n - a.shape[-1])])


def kernel(x_prompt, x_sample, state_ret, cache_mla_latent, cache_mla_krope, cache_fox_k, cache_fox_v,
           cache_fox_logf, p_prompt, p_sample, norm_mix, norm_ffn, norm_pe, norm_final, ret_w_in, ret_gn,
           ret_w_out, mla_w_in, mla_q_norm, mla_kv_norm, mla_w_qb, mla_w_kvb, mla_gq_nope, mla_gq_rope,
           mla_gk_nope, mla_gk_rope, mla_w_out, fox_w_in, fox_b_f, fox_gq, fox_gk, fox_w_out, ffn_w_gate,
           ffn_w_up, ffn_w_down, pe_w_proj, pe_w_gate):
    bf = lambda a: a.astype(BF16)
    n_ret, n_mla, n_fox = state_ret.shape[0], cache_mla_latent.shape[0], cache_fox_k.shape[0]
    mla = []
    for j in range(n_mla):
        w_qb = mla_w_qb[j].reshape(MLA_Q_LORA, MLA_HEADS, MLA_NOPE + MLA_ROPE)
        w_kvb = mla_w_kvb[j].reshape(MLA_KV_LORA, MLA_HEADS, MLA_NOPE + MLA_V)
        mla.append(dict(
            w_in=bf(_pad_last(mla_w_in[j], MLA_IN_PAD)),
            q_norm=mla_q_norm[j], kv_norm=mla_kv_norm[j], gq_nope=mla_gq_nope[j], gk_nope=mla_gk_nope[j],
            gq_rope_pad=_pad_last(mla_gq_rope[j], LANES).reshape(1, LANES),
            gk_rope_pad=_pad_last(mla_gk_rope[j], LANES).reshape(1, LANES),
            w_qb_pad=bf(_pad_last(w_qb, MLA_DK_PAD).reshape(MLA_Q_LORA, MLA_HEADS * MLA_DK_PAD)),
            w_kb=bf(w_kvb[:, :, :MLA_NOPE].reshape(MLA_KV_LORA, MLA_HEADS * MLA_NOPE)),
            w_vb=bf(w_kvb[:, :, MLA_NOPE:].reshape(MLA_KV_LORA, MLA_HEADS * MLA_V)),
            w_out=bf(mla_w_out[j])))
    fox = []
    for j in range(n_fox):
        wf = fox_w_in[j]
        zcol = jnp.zeros((D_MODEL, LANES), wf.dtype)
        fox.append(dict(
            w_in=bf(jnp.concatenate(
                [wf[:, :FOX_W], _pad_last(wf[:, 4 * FOX_W:], LANES), wf[:, FOX_W:2 * FOX_W], zcol,
                 wf[:, 2 * FOX_W:3 * FOX_W], zcol, wf[:, 3 * FOX_W:4 * FOX_W], zcol], axis=1)),
            gq2=jnp.tile(fox_gq[j], 2).reshape(1, LANES), gk2=jnp.tile(fox_gk[j], 2).reshape(1, LANES),
            bf_pad=_pad_last(fox_b_f[j], LANES).reshape(1, LANES),
            w_out=bf(fox_w_out[j])))
    w = dict(norm_mix=norm_mix, norm_ffn=norm_ffn, norm_pe=norm_pe, norm_final=norm_final,
             ret_w_in=bf(ret_w_in), ret_gn=ret_gn, ret_w_out=bf(ret_w_out), mla=mla, fox=fox,
             ffn_w_gate=bf(ffn_w_gate), ffn_w_up=bf(ffn_w_up), ffn_w_down=bf(ffn_w_down),
             pe_w_proj=bf(pe_w_proj), pe_w_gate=bf(pe_w_gate))
    bp = x_prompt.shape[0]
    dt = x_prompt.dtype
    y_p, ret_p, lat_p, kr_p, fk_p, fv_p, flf_p = _trunk(
        x_prompt, p_prompt, 0,
        jnp.zeros((n_ret, bp, RET_HEADS, RET_DK, RET_DV), dt),
        jnp.zeros((n_mla, bp, 0, MLA_KV_LORA), dt), jnp.zeros((n_mla, bp, 0, MLA_ROPE), dt),
        jnp.zeros((n_fox, bp, 0, FOX_HEADS, FOX_DH), dt), jnp.zeros((n_fox, bp, 0, FOX_HEADS, FOX_DH), dt),
        jnp.zeros((n_fox, bp, 0, FOX_HEADS), dt), w)
    y_s, ret_s, lat_s, kr_s, fk_s, fv_s, flf_s = _trunk(
        x_sample, p_sample, cache_mla_latent.shape[2], state_ret, cache_mla_latent, cache_mla_krope,
        cache_fox_k, cache_fox_v, cache_fox_logf, w)
    return (y_p, y_s, ret_p, ret_s, lat_p, kr_p, lat_s, kr_s, fk_p, fv_p, flf_p, fk_s, fv_s, flf_s)
```

```python
import functools
import math

import jax
import jax.numpy as jnp
from jax import lax
from jax.experimental import pallas as pl
from jax.experimental.pallas import tpu as pltpu

D_MODEL = 1024
DEPTH = 4
CHUNK = 64
PE_DIM = 256
N_MIXERS = 3
EPS = 1e-6
ROPE_THETA = 10000.0
RET_HEADS = 4
RET_DK = 256
RET_DV = 512
RET_QK = RET_HEADS * RET_DK
RET_VW = RET_HEADS * RET_DV
MLA_HEADS = 16
MLA_NOPE = 128
MLA_ROPE = 64
MLA_V = 128
MLA_Q_LORA = 512
MLA_KV_LORA = 256
FOX_HEADS = 16
FOX_DH = 64
FOX_W = FOX_HEADS * FOX_DH
D_FF = ((8 * D_MODEL + 3 * 256 - 1) // (3 * 256)) * 256

LANES = 128
MLA_DK_PAD = 2 * LANES
MLA_IN_PAD = 7 * LANES
PROJ_TN = 512
FOX_SEG = FOX_W + LANES
MASK_VALUE = -1e30
LOG2E = math.log2(math.e)
VMEM_LIMIT = 48 * 1024 * 1024
VMEM_LIMIT_BIG = 56 * 1024 * 1024

F32 = jnp.float32
BF16 = jnp.bfloat16
NT_DIMS = (((1,), (1,)), ((), ()))


def _cparams(*sem, vmem=VMEM_LIMIT):
    return pltpu.CompilerParams(dimension_semantics=sem, vmem_limit_bytes=vmem)


def _pick(n, prefs):
    for p in prefs:
        if n % p == 0:
            return p
    return n


def _rms_bf16(x, g):
    ms = jnp.mean(x * x, axis=-1, keepdims=True)
    return (x * lax.rsqrt(ms + EPS) * g).astype(BF16)


def _pos_table_spec(t, tm):
    if t % tm == 0:
        nt = t // tm
        return pl.BlockSpec((tm, LANES), lambda i, *_: (i % nt, 0))
    return pl.BlockSpec((tm, LANES), lambda i, *_: (0, 0))


def _pos_table(tab, t, tm):
    return tab if t % tm == 0 else jnp.tile(tab, (tm // t, 1))


def _norm_mm_kernel(x_ref, g_ref, w_ref, o_ref, xn_ref):
    @pl.when(pl.program_id(1) == 0)
    def _():
        xn_ref[...] = _rms_bf16(x_ref[...], g_ref[...])

    o_ref[...] = jnp.dot(xn_ref[...], w_ref[...], preferred_element_type=F32).astype(o_ref.dtype)


def norm_matmul(x, g, w, out_dtype=F32, name="norm_mm"):
    m, k = x.shape
    n = w.shape[1]
    tm = _pick(m, (1024, 512, 256))
    tn = _pick(n, (1024, 768, 512, 896, 128))
    return pl.pallas_call(
        _norm_mm_kernel,
        grid=(m // tm, n // tn),
        in_specs=[
            pl.BlockSpec((tm, k), lambda i, j: (i, 0)),
            pl.BlockSpec((1, k), lambda i, j: (0, 0)),
            pl.BlockSpec((k, tn), lambda i, j: (0, j)),
        ],
        out_specs=pl.BlockSpec((tm, tn), lambda i, j: (i, j)),
        out_shape=jax.ShapeDtypeStruct((m, n), out_dtype),
        scratch_shapes=[pltpu.VMEM((tm, k), BF16)],
        compiler_params=_cparams("parallel", "arbitrary"),
        name=name,
    )(x, g.reshape(1, k), w)


def _mm_res_kernel(a_ref, w_ref, r_ref, o_ref):
    o_ref[...] = r_ref[...] + jnp.dot(a_ref[...], w_ref[...], preferred_element_type=F32)


def matmul_res(a, w, res, name="mm"):
    m, k = a.shape
    n = w.shape[1]
    tm = _pick(m, (1024, 512, 256))
    tn = _pick(n, (1024, 512, 128))
    return pl.pallas_call(
        _mm_res_kernel,
        grid=(m // tm, n // tn),
        in_specs=[
            pl.BlockSpec((tm, k), lambda i, j: (i, 0)),
            pl.BlockSpec((k, tn), lambda i, j: (0, j)),
            pl.BlockSpec((tm, tn), lambda i, j: (i, j)),
        ],
        out_specs=pl.BlockSpec((tm, tn), lambda i, j: (i, j)),
        out_shape=jax.ShapeDtypeStruct((m, n), F32),
        compiler_params=_cparams("parallel", "arbitrary"),
        name=name,
    )(a, w, res)


def _ffn_pe_kernel(x_ref, p_ref, g_ref, wg_ref, wu_ref, wd_ref, gp_ref, wpg_ref, wpp_ref, *rest, final):
    o_ref, xn_ref = rest[-2], rest[-1]
    f = pl.program_id(1)

    @pl.when(f == 0)
    def _():
        x = x_ref[...]
        xn_ref[...] = _rms_bf16(x, g_ref[...])
        o_ref[...] = x

    xn = xn_ref[...]
    tf = wg_ref.shape[1]
    for lo in range(0, tf, PROJ_TN):
        wd_rows = min(PROJ_TN, tf - lo)
        gate = jnp.dot(xn, wg_ref[:, lo:lo + wd_rows], preferred_element_type=F32)
        up = jnp.dot(xn, wu_ref[:, lo:lo + wd_rows], preferred_element_type=F32)
        act = (gate * jax.nn.sigmoid(gate) * up).astype(BF16)
        o_ref[...] += jnp.dot(act, wd_ref[lo:lo + wd_rows, :], preferred_element_type=F32)

    @pl.when(f == pl.num_programs(1) - 1)
    def _():
        h = o_ref[...]
        xn_ref[...] = _rms_bf16(h, gp_ref[...])
        pb = p_ref[...].astype(BF16)
        for lo in range(0, h.shape[1], PROJ_TN):
            gate = jax.nn.sigmoid(jnp.dot(xn_ref[...], wpg_ref[:, lo:lo + PROJ_TN], preferred_element_type=F32))
            proj = jnp.dot(pb, wpp_ref[:, lo:lo + PROJ_TN], preferred_element_type=F32)
            o_ref[:, lo:lo + PROJ_TN] = h[:, lo:lo + PROJ_TN] + gate * proj
        if final:
            out = o_ref[...]
            ms = jnp.mean(out * out, axis=-1, keepdims=True)
            o_ref[...] = out * lax.rsqrt(ms + EPS) * rest[0][...]


def ffn_pe(h, p, layer, g, wg, wu, wd, gp, wpg, wpp, final_gain=None):
    m, d = h.shape
    dff = wg.shape[1]
    pd = p.shape[2]
    tm = _pick(m, (1024, 512))
    tf = dff // 2
    row = pl.BlockSpec((1, d), lambda i, f: (0, 0))
    in_specs = [
        pl.BlockSpec((tm, d), lambda i, f: (i, 0)),
        pl.BlockSpec((None, tm, pd), lambda i, f: (layer, i, 0)),
        row,
        pl.BlockSpec((d, tf), lambda i, f: (0, f)),
        pl.BlockSpec((d, tf), lambda i, f: (0, f)),
        pl.BlockSpec((tf, d), lambda i, f: (f, 0)),
        row,
        pl.BlockSpec((d, d), lambda i, f: (0, 0)),
        pl.BlockSpec((pd, d), lambda i, f: (0, 0)),
    ]
    args = [h, p, g.reshape(1, d), wg, wu, wd, gp.reshape(1, d), wpg, wpp]
    if final_gain is not None:
        in_specs.append(row)
        args.append(final_gain.reshape(1, d))
    return pl.pallas_call(
        functools.partial(_ffn_pe_kernel, final=final_gain is not None),
        grid=(m // tm, dff // tf),
        in_specs=in_specs,
        out_specs=pl.BlockSpec((tm, d), lambda i, f: (i, 0)),
        out_shape=jax.ShapeDtypeStruct((m, d), F32),
        scratch_shapes=[pltpu.VMEM((tm, d), BF16)],
        compiler_params=_cparams("parallel", "arbitrary", vmem=VMEM_LIMIT_BIG),
        name="ffn_pe",
    )(*args)


def _attn_kernel(*refs, tq, nq, tkp, mw, hpb, has_bias, has_gate):
    q_ref, k_ref, v_ref, mb_ref = refs[:4]
    pos = 4
    if has_bias:
        fq_ref, fk_ref = refs[4:6]
        pos = 6
    if has_gate:
        g_ref = refs[pos]
        pos += 1
    o_ref = refs[pos]
    bw_k = q_ref.shape[2]
    bw_v = v_ref.shape[2]
    mb = mb_ref[...]
    lane_k = lax.broadcasted_iota(jnp.int32, (tq, bw_k), 1)
    lane_v = lax.broadcasted_iota(jnp.int32, (tq, bw_v), 1)

    for qi in range(nq):
        r0 = qi * tq
        kv_len = tkp - (nq - 1 - qi) * tq
        head_len = kv_len - mw
        q_blk = q_ref[0, r0:r0 + tq, :]
        out = None
        for hh in range(hpb):
            q = q_blk if hpb == 1 else jnp.where(lane_k // (bw_k // hpb) == hh, q_blk, jnp.zeros_like(q_blk))

            head = pl.program_id(1) * hpb + hh

            def scores(lo, hi):
                s = lax.dot_general(q, k_ref[0, lo:hi, :], NT_DIMS, preferred_element_type=F32)
                if has_bias:
                    s = s - fk_ref[0, hh, :, lo:hi]
                return s

            s_tail = scores(head_len, kv_len) + mb
            m = jnp.max(s_tail, axis=1, keepdims=True)
            if head_len > 0:
                s_head = scores(0, head_len)
                m = jnp.maximum(m, jnp.max(s_head, axis=1, keepdims=True))
            if has_bias:
                f_rows = fq_ref[0, r0:r0 + tq, :]
                lane_f = lax.broadcasted_iota(jnp.int32, f_rows.shape, 1)
                fq = jnp.sum(jnp.where(lane_f == head, f_rows, 0.0), axis=1, keepdims=True)
                shift = fq - (fq + m)
            else:
                shift = -m
            p_tail = jnp.exp2(s_tail + shift)
            l = jnp.sum(p_tail, axis=1, keepdims=True)
            pv = jnp.dot(p_tail.astype(BF16), v_ref[0, head_len:kv_len, :], preferred_element_type=F32)
            if head_len > 0:
                p_head = jnp.exp2(s_head + shift)
                l = l + jnp.sum(p_head, axis=1, keepdims=True)
                pv = pv + jnp.dot(p_head.astype(BF16), v_ref[0, 0:head_len, :], preferred_element_type=F32)
            o_h = pv * (1.0 / l)
            out = o_h if hh == 0 else jnp.where(lane_v // (bw_v // hpb) == hh, o_h, out)
        if has_gate:
            out = jax.nn.sigmoid(g_ref[0, r0:r0 + tq, :].astype(F32)) * out
        o_ref[0, r0:r0 + tq, :] = out.astype(o_ref.dtype)


def attention(q, k, v, *, mode, n_blocks, hpb, tk_real, name, fq=None, fk=None, gate=None):
    b, tq_all, wq = q.shape
    tkp = k.shape[1]
    bw_k = wq // n_blocks
    bw_v = v.shape[2] // n_blocks
    tq = _pick(tq_all, (512,))
    nq = tq_all // tq
    mw = tq if nq > 1 or tq_all == tkp else LANES
    q0 = tk_real - tq
    k0 = tkp - mw
    qpos = q0 + lax.broadcasted_iota(jnp.int32, (tq, mw), 0)
    kpos = k0 + lax.broadcasted_iota(jnp.int32, (tq, mw), 1)
    allowed = (kpos // CHUNK <= qpos // CHUNK) if mode == "chunk" else (kpos <= qpos)
    allowed = jnp.logical_and(allowed, kpos < tk_real)
    mb = jnp.where(allowed, 0.0, MASK_VALUE).astype(F32)

    in_specs = [
        pl.BlockSpec((1, tq_all, bw_k), lambda bi, hi: (bi, 0, hi)),
        pl.BlockSpec((1, tkp, bw_k), lambda bi, hi: (bi, 0, hi)),
        pl.BlockSpec((1, tkp, bw_v), lambda bi, hi: (bi, 0, hi)),
        pl.BlockSpec((tq, mw), lambda bi, hi: (0, 0)),
    ]
    args = [q, k, v, mb]
    if fq is not None:
        in_specs += [
            pl.BlockSpec((1, tq_all, LANES), lambda bi, hi: (bi, 0, 0)),
            pl.BlockSpec((1, hpb, 1, tkp), lambda bi, hi: (bi, hi, 0, 0)),
        ]
        args += [fq, fk]
    if gate is not None:
        in_specs.append(pl.BlockSpec((1, tq_all, bw_v), lambda bi, hi: (bi, 0, hi)))
        args.append(gate)
    kern = functools.partial(_attn_kernel, tq=tq, nq=nq, tkp=tkp, mw=mw, hpb=hpb,
                             has_bias=fq is not None, has_gate=gate is not None)
    return pl.pallas_call(
        kern,
        grid=(b, n_blocks),
        in_specs=in_specs,
        out_specs=pl.BlockSpec((1, tq_all, bw_v), lambda bi, hi: (bi, 0, hi)),
        out_shape=jax.ShapeDtypeStruct((b, tq_all, n_blocks * bw_v), BF16),
        compiler_params=_cparams("parallel", "parallel"),
        name=name,
    )(*args)


def _ret_in_kernel(x_ref, g_ref, w_ref, cos_ref, sin_ref, q_ref, k_ref, v_ref, gate_ref, xn_ref):
    j = pl.program_id(1)
    nv = RET_VW // RET_QK

    @pl.when(j == 0)
    def _():
        xn_ref[...] = _rms_bf16(x_ref[...], g_ref[...])

    def chunks():
        for lo in range(0, RET_QK, PROJ_TN):
            yield lo, jnp.dot(xn_ref[...], w_ref[:, lo:lo + PROJ_TN], preferred_element_type=F32)

    def rope_store(dst, mult):
        cos = cos_ref[...]
        sin = sin_ref[...]
        half = RET_DK // 2
        for lo, acc in chunks():
            for hh in range(PROJ_TN // RET_DK):
                c0 = hh * RET_DK
                x1 = acc[:, c0:c0 + half]
                x2 = acc[:, c0 + half:c0 + RET_DK]
                dst[:, lo + c0:lo + c0 + half] = ((x1 * cos - x2 * sin) * mult).astype(BF16)
                dst[:, lo + c0 + half:lo + c0 + RET_DK] = ((x2 * cos + x1 * sin) * mult).astype(BF16)

    def plain_store(dst):
        for lo, acc in chunks():
            dst[:, lo:lo + PROJ_TN] = acc.astype(BF16)

    @pl.when(j == 0)
    def _():
        rope_store(q_ref, 1.0)

    @pl.when(j == 1)
    def _():
        rope_store(k_ref, RET_DK ** -0.5)

    @pl.when(jnp.logical_and(j >= 2, j < 2 + nv))
    def _():
        plain_store(v_ref)

    @pl.when(j >= 2 + nv)
    def _():
        plain_store(gate_ref)


def ret_in(x, g, w, cos, sin, t):
    m, d = x.shape
    tm = _pick(m, (1024, 512))
    nv = RET_VW // RET_QK
    nj = 2 + 2 * nv
    clip = lambda j, lo, n: jnp.clip(j - lo, 0, n - 1)
    tab = _pos_table_spec(t, tm)
    return pl.pallas_call(
        _ret_in_kernel,
        grid=(m // tm, nj),
        in_specs=[
            pl.BlockSpec((tm, d), lambda i, j: (i, 0)),
            pl.BlockSpec((1, d), lambda i, j: (0, 0)),
            pl.BlockSpec((d, RET_QK), lambda i, j: (0, j)),
            tab, tab,
        ],
        out_specs=[
            pl.BlockSpec((tm, RET_QK), lambda i, j: (i, 0)),
            pl.BlockSpec((tm, RET_QK), lambda i, j: (i, 0)),
            pl.BlockSpec((tm, RET_QK), lambda i, j: (i, clip(j, 2, nv))),
            pl.BlockSpec((tm, RET_QK), lambda i, j: (i, clip(j, 2 + nv, nv))),
        ],
        out_shape=[
            jax.ShapeDtypeStruct((m, RET_QK), BF16),
            jax.ShapeDtypeStruct((m, RET_QK), BF16),
            jax.ShapeDtypeStruct((m, RET_VW), BF16),
            jax.ShapeDtypeStruct((m, RET_VW), BF16),
        ],
        scratch_shapes=[pltpu.VMEM((tm, d), BF16)],
        compiler_params=_cparams("parallel", "arbitrary", vmem=VMEM_LIMIT_BIG),
        name="ret_in",
    )(x, g.reshape(1, d), w, _pos_table(cos, t, tm), _pos_table(sin, t, tm))


def _retention_kernel(lg_ref, q_ref, k_ref, v_ref, gate_ref, gn_ref, s0_ref, a_ref, s_ref, *, chunk, nc):
    lg = lg_ref[pl.program_id(1)]
    row = lax.broadcasted_iota(jnp.int32, (chunk, chunk), 0)
    col = lax.broadcasted_iota(jnp.int32, (chunk, chunk), 1)
    diff = row - col
    dmask = jnp.where(diff >= 0, jnp.exp(lg * jnp.maximum(diff, 0).astype(F32)), 0.0)
    idx = lax.broadcasted_iota(jnp.int32, (chunk, 1), 0).astype(F32)
    q_dec = jnp.exp(lg * (idx + 1.0))
    k_dec = jnp.exp(lg * (chunk - 1.0 - idx))
    c_dec = jnp.exp(lg * jnp.full((1, RET_DV), float(chunk), F32))
    gn = gn_ref[...]

    state = s0_ref[0, 0]
    for c in range(nc):
        rows = slice(c * chunk, (c + 1) * chunk)
        q = q_ref[0, rows, :]
        k = k_ref[0, rows, :]
        v = v_ref[0, rows, :]
        att = lax.dot_general(q, k, NT_DIMS, preferred_element_type=F32) * dmask
        qd = (q.astype(F32) * q_dec).astype(BF16)
        o = jnp.dot(att.astype(BF16), v, preferred_element_type=F32)
        o = o + jnp.dot(qd, state.astype(BF16), preferred_element_type=F32)
        kd = (k.astype(F32) * k_dec).astype(BF16)
        upd = lax.dot_general(kd, v, (((0,), (0,)), ((), ())), preferred_element_type=F32)
        state = state * c_dec + upd

        mu = jnp.mean(o, axis=-1, keepdims=True)
        oc = o - mu
        var = jnp.mean(oc * oc, axis=-1, keepdims=True)
        y = oc * lax.rsqrt(var + EPS) * gn
        g = gate_ref[0, rows, :].astype(F32)
        a_ref[0, rows, :] = (g * jax.nn.sigmoid(g) * y).astype(BF16)
    s_ref[0, 0] = state


def retention(q, k, v, gate, gn, s0, lg, chunk):
    b, t, _ = q.shape
    nc = t // chunk
    kern = functools.partial(_retention_kernel, chunk=chunk, nc=nc)
    tok = lambda bi, hi, lg: (bi, 0, hi)
    st = lambda bi, hi, lg: (bi, hi, 0, 0)
    return pl.pallas_call(
        kern,
        grid_spec=pltpu.PrefetchScalarGridSpec(
            num_scalar_prefetch=1,
            grid=(b, RET_HEADS),
            in_specs=[
                pl.BlockSpec((1, t, RET_DK), tok),
                pl.BlockSpec((1, t, RET_DK), tok),
                pl.BlockSpec((1, t, RET_DV), tok),
                pl.BlockSpec((1, t, RET_DV), tok),
                pl.BlockSpec((1, RET_DV), lambda bi, hi, lg: (0, hi)),
                pl.BlockSpec((1, 1, RET_DK, RET_DV), st),
            ],
            out_specs=[
                pl.BlockSpec((1, t, RET_DV), tok),
                pl.BlockSpec((1, 1, RET_DK, RET_DV), st),
            ],
        ),
        out_shape=[
            jax.ShapeDtypeStruct((b, t, RET_VW), BF16),
            jax.ShapeDtypeStruct((b, RET_HEADS, RET_DK, RET_DV), F32),
        ],
        compiler_params=_cparams("parallel", "parallel"),
        name="retention",
    )(lg, q, k, v, gate, gn, s0)


def _rope_lanes(x, c_tab, s1_tab, s2_tab):
    return x * c_tab + pltpu.roll(x, 3 * LANES // 4, 1) * s1_tab + pltpu.roll(x, LANES // 4, 1) * s2_tab


def _mla_q_kernel(p_ref, qn_ref, kvn_ref, gqn_ref, gqr_ref, gkr_ref, w_ref, c_ref, s1_ref, s2_ref,
                  q_ref, lat_ref, kr_ref):
    c_tab, s1_tab, s2_tab = c_ref[...], s1_ref[...], s2_ref[...]

    ckv = p_ref[:, MLA_Q_LORA:MLA_Q_LORA + MLA_KV_LORA]
    lat_ref[...] = ckv * lax.rsqrt(jnp.mean(ckv * ckv, axis=-1, keepdims=True) + EPS) * kvn_ref[...]

    krp = p_ref[:, MLA_Q_LORA + MLA_KV_LORA:MLA_IN_PAD]
    krn = krp * lax.rsqrt(jnp.sum(krp * krp, axis=-1, keepdims=True) * (1.0 / MLA_ROPE) + EPS) * gkr_ref[...]
    kr_ref[...] = _rope_lanes(krn, c_tab, s1_tab, s2_tab)

    cqn = _rms_bf16(p_ref[:, 0:MLA_Q_LORA], qn_ref[...])
    hg = 4
    for g0 in range(0, MLA_HEADS, hg):
        acc = jnp.dot(cqn, w_ref[:, g0 * MLA_DK_PAD:(g0 + hg) * MLA_DK_PAD], preferred_element_type=F32)
        for hh in range(hg):
            a = acc[:, hh * MLA_DK_PAD:hh * MLA_DK_PAD + LANES]
            r = acc[:, hh * MLA_DK_PAD + LANES:(hh + 1) * MLA_DK_PAD]
            an = a * lax.rsqrt(jnp.mean(a * a, axis=-1, keepdims=True) + EPS) * gqn_ref[...]
            rn = r * lax.rsqrt(jnp.sum(r * r, axis=-1, keepdims=True) * (1.0 / MLA_ROPE) + EPS) * gqr_ref[...]
            col = (g0 + hh) * MLA_DK_PAD
            q_ref[:, col:col + LANES] = an.astype(BF16)
            q_ref[:, col + LANES:col + MLA_DK_PAD] = _rope_lanes(rn, c_tab, s1_tab, s2_tab).astype(BF16)


def mla_q(proj, q_norm, kv_norm, gq_nope, gq_rope_pad, gk_rope_pad, w_qb_pad, tabs, t):
    m = proj.shape[0]
    tm = _pick(m, (512, 256))
    row = lambda n: pl.BlockSpec((1, n), lambda i: (0, 0))
    tab = _pos_table_spec(t, tm)
    return pl.pallas_call(
        _mla_q_kernel,
        grid=(m // tm,),
        in_specs=[
            pl.BlockSpec((tm, MLA_IN_PAD), lambda i: (i, 0)),
            row(MLA_Q_LORA), row(MLA_KV_LORA), row(LANES), row(LANES), row(LANES),
            pl.BlockSpec((MLA_Q_LORA, MLA_HEADS * MLA_DK_PAD), lambda i: (0, 0)),
            tab, tab, tab,
        ],
        out_specs=[
            pl.BlockSpec((tm, MLA_HEADS * MLA_DK_PAD), lambda i: (i, 0)),
            pl.BlockSpec((tm, MLA_KV_LORA), lambda i: (i, 0)),
            pl.BlockSpec((tm, LANES), lambda i: (i, 0)),
        ],
        out_shape=[
            jax.ShapeDtypeStruct((m, MLA_HEADS * MLA_DK_PAD), BF16),
            jax.ShapeDtypeStruct((m, MLA_KV_LORA), F32),
            jax.ShapeDtypeStruct((m, LANES), F32),
        ],
        compiler_params=_cparams("parallel"),
        name="mla_q",
    )(proj, q_norm.reshape(1, -1), kv_norm.reshape(1, -1), gq_nope.reshape(1, -1), gq_rope_pad, gk_rope_pad,
      w_qb_pad, *[_pos_table(x, t, tm) for x in tabs])


def _mla_kv_kernel(lat_ref, kr_ref, gk_ref, wk_ref, wv_ref, k_ref, v_ref):
    latb = lat_ref[...].astype(BF16)
    krb = kr_ref[...].astype(BF16)
    v_ref[...] = jnp.dot(latb, wv_ref[...], preferred_element_type=F32).astype(BF16)
    hg = 4
    for g0 in range(0, MLA_HEADS, hg):
        acc = jnp.dot(latb, wk_ref[:, g0 * MLA_NOPE:(g0 + hg) * MLA_NOPE], preferred_element_type=F32)
        for hh in range(hg):
            a = acc[:, hh * MLA_NOPE:(hh + 1) * MLA_NOPE]
            an = a * lax.rsqrt(jnp.mean(a * a, axis=-1, keepdims=True) + EPS) * gk_ref[...]
            col = (g0 + hh) * MLA_DK_PAD
            k_ref[:, col:col + LANES] = an.astype(BF16)
            k_ref[:, col + LANES:col + MLA_DK_PAD] = krb


def mla_kv(lat_all, kr_all, gk_nope, wk, wv):
    m = lat_all.shape[0]
    tm = _pick(m, (512, 256))
    return pl.pallas_call(
        _mla_kv_kernel,
        grid=(m // tm,),
        in_specs=[
            pl.BlockSpec((tm, MLA_KV_LORA), lambda i: (i, 0)),
            pl.BlockSpec((tm, LANES), lambda i: (i, 0)),
            pl.BlockSpec((1, MLA_NOPE), lambda i: (0, 0)),
            pl.BlockSpec((MLA_KV_LORA, MLA_HEADS * MLA_NOPE), lambda i: (0, 0)),
            pl.BlockSpec((MLA_KV_LORA, MLA_HEADS * MLA_V), lambda i: (0, 0)),
        ],
        out_specs=[
            pl.BlockSpec((tm, MLA_HEADS * MLA_DK_PAD), lambda i: (i, 0)),
            pl.BlockSpec((tm, MLA_HEADS * MLA_V), lambda i: (i, 0)),
        ],
        out_shape=[
            jax.ShapeDtypeStruct((m, MLA_HEADS * MLA_DK_PAD), BF16),
            jax.ShapeDtypeStruct((m, MLA_HEADS * MLA_V), BF16),
        ],
        compiler_params=_cparams("parallel"),
        name="mla_kv",
    )(lat_all, kr_all, gk_nope.reshape(1, -1), wk, wv)


def _head_norm(x, gain):
    lane = lax.broadcasted_iota(jnp.int32, (x.shape[0], LANES), 1)
    low = lane < FOX_DH
    outs = []
    for c in range(x.shape[1] // LANES):
        xb = x[:, c * LANES:(c + 1) * LANES]
        sq = xb * xb
        s_lo = jnp.sum(jnp.where(low, sq, 0.0), axis=-1, keepdims=True)
        s_hi = jnp.sum(jnp.where(low, 0.0, sq), axis=-1, keepdims=True)
        ms = jnp.where(low, s_lo, s_hi) * (1.0 / FOX_DH)
        outs.append(xb * lax.rsqrt(ms + EPS) * gain)
    return outs


def _fox_in_kernel(x_ref, g_ref, w_ref, gq_ref, gk_ref, bf_ref,
                   q_ref, k_ref, kb_ref, v_ref, vb_ref, gate_ref, lf_ref, xn_ref):
    j = pl.program_id(1)

    @pl.when(j == 0)
    def _():
        xn_ref[...] = _rms_bf16(x_ref[...], g_ref[...])

    def chunks():
        for lo in range(0, FOX_W, PROJ_TN):
            yield lo, jnp.dot(xn_ref[...], w_ref[:, lo:lo + PROJ_TN], preferred_element_type=F32)

    @pl.when(j == 0)
    def _():
        for lo, acc in chunks():
            for c, blk in enumerate(_head_norm(acc, gq_ref[...])):
                q_ref[:, lo + c * LANES:lo + (c + 1) * LANES] = (blk * (FOX_DH ** -0.5 * LOG2E)).astype(BF16)
        fl = jnp.dot(xn_ref[...], w_ref[:, FOX_W:FOX_SEG], preferred_element_type=F32)
        lf_ref[...] = jax.nn.log_sigmoid(fl + bf_ref[...])

    @pl.when(j == 1)
    def _():
        for lo, acc in chunks():
            for c, blk in enumerate(_head_norm(acc, gk_ref[...])):
                k_ref[:, lo + c * LANES:lo + (c + 1) * LANES] = blk
                kb_ref[:, lo + c * LANES:lo + (c + 1) * LANES] = blk.astype(BF16)

    @pl.when(j == 2)
    def _():
        for lo, acc in chunks():
            v_ref[:, lo:lo + PROJ_TN] = acc
            vb_ref[:, lo:lo + PROJ_TN] = acc.astype(BF16)

    @pl.when(j == 3)
    def _():
        for lo, acc in chunks():
            gate_ref[:, lo:lo + PROJ_TN] = acc.astype(BF16)


def fox_in(x, g, w, gq2, gk2, bf_pad):
    m, d = x.shape
    tm = _pick(m, (512, 256))
    seg = lambda: pl.BlockSpec((tm, FOX_W), lambda i, j: (i, 0))
    row = pl.BlockSpec((1, LANES), lambda i, j: (0, 0))
    return pl.pallas_call(
        _fox_in_kernel,
        grid=(m // tm, 4),
        in_specs=[
            pl.BlockSpec((tm, d), lambda i, j: (i, 0)),
            pl.BlockSpec((1, d), lambda i, j: (0, 0)),
            pl.BlockSpec((d, FOX_SEG), lambda i, j: (0, j)),
            row, row, row,
        ],
        out_specs=[seg(), seg(), seg(), seg(), seg(), seg(),
                   pl.BlockSpec((tm, LANES), lambda i, j: (i, 0))],
        out_shape=[
            jax.ShapeDtypeStruct((m, FOX_W), BF16),
            jax.ShapeDtypeStruct((m, FOX_W), F32),
            jax.ShapeDtypeStruct((m, FOX_W), BF16),
            jax.ShapeDtypeStruct((m, FOX_W), F32),
            jax.ShapeDtypeStruct((m, FOX_W), BF16),
            jax.ShapeDtypeStruct((m, FOX_W), BF16),
            jax.ShapeDtypeStruct((m, LANES), F32),
        ],
        scratch_shapes=[pltpu.VMEM((tm, d), BF16)],
        compiler_params=_cparams("parallel", "arbitrary"),
        name="fox_in",
    )(x, g.reshape(1, d), w, gq2, gk2, bf_pad)


def _fox_cumsum_kernel(lf_ref, ft_ref, fh_ref):
    tkp = lf_ref.shape[1]
    x = lf_ref[0].T[0:FOX_HEADS, :]
    lane = lax.broadcasted_iota(jnp.int32, x.shape, 1)
    shift = 1
    while shift < tkp:
        x = x + jnp.where(lane >= shift, pltpu.roll(x, shift, 1), 0.0)
        shift *= 2
    x = x * LOG2E
    for h in range(FOX_HEADS):
        fh_ref[0, h] = x[h:h + 1, :]
    ft_ref[0] = jnp.concatenate([x, jnp.zeros((LANES - FOX_HEADS, tkp), F32)], axis=0).T


def fox_cumsum(lf_all):
    b, tkp, _ = lf_all.shape
    return pl.pallas_call(
        _fox_cumsum_kernel,
        grid=(b,),
        in_specs=[pl.BlockSpec((1, tkp, LANES), lambda i: (i, 0, 0))],
        out_specs=[pl.BlockSpec((1, tkp, LANES), lambda i: (i, 0, 0)),
                   pl.BlockSpec((1, FOX_HEADS, 1, tkp), lambda i: (i, 0, 0, 0))],
        out_shape=[jax.ShapeDtypeStruct((b, tkp, LANES), F32),
                   jax.ShapeDtypeStruct((b, FOX_HEADS, 1, tkp), F32)],
        compiler_params=_cparams("parallel"),
        name="fox_cumsum",
    )(lf_all)


def _rope_tables(pos, half):
    inv = ROPE_THETA ** (-jnp.arange(half, dtype=F32) / half)
    ang = pos.astype(F32)[:, None] * inv[None, :]
    return jnp.cos(ang), jnp.sin(ang)


def _pad_rows(a, rows):
    pad = rows - a.shape[1]
    if pad == 0:
        return a
    return jnp.pad(a, ((0, 0), (0, pad)) + ((0, 0),) * (a.ndim - 2))


def _retention_mixer(h2, b, t, pos, s0, gmix, w_in, gn, w_out):
    cos, sin = _rope_tables(pos, RET_DK // 2)
    q, k, v, gate = ret_in(h2, gmix, w_in, cos, sin, t)
    lg = jnp.log1p(-jnp.exp2(-5.0 - jnp.arange(RET_HEADS, dtype=F32)))
    chunk = _pick(t, (256, 128, 64))
    a, s = retention(q.reshape(b, t, RET_QK), k.reshape(b, t, RET_QK), v.reshape(b, t, RET_VW),
                     gate.reshape(b, t, RET_VW), gn.reshape(1, RET_VW), s0, lg, chunk)
    return matmul_res(a.reshape(b * t, RET_VW), w_out, h2, name="ret_out"), s


def _mla_mixer(h2, b, t, pos, lat_past, kr_past, gmix, w):
    proj = norm_matmul(h2, gmix, w['w_in'], name="mla_in")
    cos, sin = _rope_tables(pos, MLA_ROPE // 2)
    z = jnp.zeros_like(cos)
    tabs = (jnp.concatenate([cos, cos, z, z], axis=1), jnp.concatenate([-sin, z, z, z], axis=1),
            jnp.concatenate([z, sin, z, z], axis=1))
    q_pad, lat, kr_pad = mla_q(proj, w['q_norm'], w['kv_norm'], w['gq_nope'], w['gq_rope_pad'], w['gk_rope_pad'],
                               w['w_qb_pad'], tabs, t)
    tk = lat_past.shape[1] + t
    tkp = -(-tk // LANES) * LANES
    if lat_past.shape[1] == 0:
        lat_all, kr_all = lat, kr_pad
    else:
        lat_all = _pad_rows(jnp.concatenate([lat_past, lat.reshape(b, t, MLA_KV_LORA)], axis=1), tkp)
        kr_past_pad = jnp.pad(kr_past, ((0, 0), (0, 0), (0, LANES - MLA_ROPE)))
        kr_all = _pad_rows(jnp.concatenate([kr_past_pad, kr_pad.reshape(b, t, LANES)], axis=1), tkp)
        lat_all = lat_all.reshape(b * tkp, MLA_KV_LORA)
        kr_all = kr_all.reshape(b * tkp, LANES)
    k_pad, v = mla_kv(lat_all, kr_all, w['gk_nope'], w['w_kb'], w['w_vb'])
    o = attention(q_pad.reshape(b, t, -1), k_pad.reshape(b, tkp, -1), v.reshape(b, tkp, -1), mode="chunk",
                  n_blocks=MLA_HEADS, hpb=1, tk_real=tk, name="mla_attn")
    h2 = matmul_res(o.reshape(b * t, MLA_HEADS * MLA_V), w['w_out'], h2, name="mla_out")
    return h2, lat.reshape(b, t, MLA_KV_LORA), kr_pad[:, :MLA_ROPE].reshape(b, t, MLA_ROPE)


def _fox_mixer(h2, b, t, k_past, v_past, lf_past, gmix, w):
    q, k, kb, v, vb, gate, lf_pad = fox_in(h2, gmix, w['w_in'], w['gq2'], w['gk2'], w['bf_pad'])
    lf = lf_pad[:, :FOX_HEADS].reshape(b, t, FOX_HEADS)
    tp = k_past.shape[1]
    tk = tp + t
    tkp = -(-tk // LANES) * LANES
    kb = kb.reshape(b, t, FOX_W)
    vb = vb.reshape(b, t, FOX_W)
    if tp > 0:
        kb = _pad_rows(jnp.concatenate([k_past.reshape(b, tp, FOX_W).astype(BF16), kb], axis=1), tkp)
        vb = _pad_rows(jnp.concatenate([v_past.reshape(b, tp, FOX_W).astype(BF16), vb], axis=1), tkp)
    lf_all = lf_pad.reshape(b, t, LANES)
    if tp > 0:
        lf_all = _pad_rows(jnp.concatenate([_pad_last(lf_past.astype(F32), LANES), lf_all], axis=1), tkp)
    f_time, fk = fox_cumsum(lf_all)
    fq = f_time[:, tk - t:tk]
    a = attention(q.reshape(b, t, FOX_W), kb, vb, mode="token", n_blocks=FOX_HEADS // 2, hpb=2, tk_real=tk,
                  name="fox_attn", fq=fq, fk=fk, gate=gate.reshape(b, t, FOX_W))
    h2 = matmul_res(a.reshape(b * t, FOX_W), w['w_out'], h2, name="fox_out")
    return (h2, k.reshape(b, t, FOX_HEADS, FOX_DH), v.reshape(b, t, FOX_HEADS, FOX_DH), lf)


def _trunk(x, p, past_len, ret_states, lat_past, kr_past, fk_past, fv_past, flf_past, w):
    b, t, d = x.shape
    pos = past_len + jnp.arange(t)
    h2 = x.reshape(b * t, d)
    new_ret, new_lat, new_kr, new_fk, new_fv, new_flf = [], [], [], [], [], []
    for i in range(DEPTH):
        kind, j = i % N_MIXERS, i // N_MIXERS
        if kind == 0:
            h2, s = _retention_mixer(h2, b, t, pos, ret_states[j], w['norm_mix'][i], w['ret_w_in'][j],
                                     w['ret_gn'][j], w['ret_w_out'][j])
            new_ret.append(s)
        elif kind == 1:
            h2, lat, kr = _mla_mixer(h2, b, t, pos, lat_past[j], kr_past[j], w['norm_mix'][i], w['mla'][j])
            new_lat.append(lat)
            new_kr.append(kr)
        else:
            h2, fk, fv, flf = _fox_mixer(h2, b, t, fk_past[j], fv_past[j], flf_past[j], w['norm_mix'][i],
                                         w['fox'][j])
            new_fk.append(fk)
            new_fv.append(fv)
            new_flf.append(flf)
        h2 = ffn_pe(h2, p.reshape(DEPTH, b * t, PE_DIM), i, w['norm_ffn'][i], w['ffn_w_gate'][i],
                    w['ffn_w_up'][i], w['ffn_w_down'][i], w['norm_pe'][i], w['pe_w_gate'][i], w['pe_w_proj'][i],
                    final_gain=w['norm_final'] if i == DEPTH - 1 else None)
    return (h2.reshape(b, t, d), jnp.stack(new_ret), jnp.stack(new_lat), jnp.stack(new_kr),
            jnp.stack(new_fk), jnp.stack(new_fv), jnp.stack(new_flf))


def _pad_last(a, n):
    return jnp.pad(a, [(0, 0)] * (a.ndim - 1) + [(0, n - a.shape[-1])])


def kernel(x_prompt, x_sample, state_ret, cache_mla_latent, cache_mla_krope, cache_fox_k, cache_fox_v,
           cache_fox_logf, p_prompt, p_sample, norm_mix, norm_ffn, norm_pe, norm_final, ret_w_in, ret_gn,
           ret_w_out, mla_w_in, mla_q_norm, mla_kv_norm, mla_w_qb, mla_w_kvb, mla_gq_nope, mla_gq_rope,
           mla_gk_nope, mla_gk_rope, mla_w_out, fox_w_in, fox_b_f, fox_gq, fox_gk, fox_w_out, ffn_w_gate,
           ffn_w_up, ffn_w_down, pe_w_proj, pe_w_gate):
    bf = lambda a: a.astype(BF16)
    n_ret, n_mla, n_fox = state_ret.shape[0], cache_mla_latent.shape[0], cache_fox_k.shape[0]
    mla = []
    qscale = (MLA_NOPE + MLA_ROPE) ** -0.5 * LOG2E
    for j in range(n_mla):
        w_qb = mla_w_qb[j].reshape(MLA_Q_LORA, MLA_HEADS, MLA_NOPE + MLA_ROPE)
        w_kvb = mla_w_kvb[j].reshape(MLA_KV_LORA, MLA_HEADS, MLA_NOPE + MLA_V)
        mla.append(dict(
            w_in=bf(_pad_last(mla_w_in[j], MLA_IN_PAD)),
            q_norm=mla_q_norm[j], kv_norm=mla_kv_norm[j], gk_nope=mla_gk_nope[j],
            gq_nope=mla_gq_nope[j] * qscale,
            gq_rope_pad=_pad_last(mla_gq_rope[j] * qscale, LANES).reshape(1, LANES),
            gk_rope_pad=_pad_last(mla_gk_rope[j], LANES).reshape(1, LANES),
            w_qb_pad=bf(_pad_last(w_qb, MLA_DK_PAD).reshape(MLA_Q_LORA, MLA_HEADS * MLA_DK_PAD)),
            w_kb=bf(w_kvb[:, :, :MLA_NOPE].reshape(MLA_KV_LORA, MLA_HEADS * MLA_NOPE)),
            w_vb=bf(w_kvb[:, :, MLA_NOPE:].reshape(MLA_KV_LORA, MLA_HEADS * MLA_V)),
            w_out=bf(mla_w_out[j])))
    fox = []
    for j in range(n_fox):
        wf = fox_w_in[j]
        zcol = jnp.zeros((D_MODEL, LANES), wf.dtype)
        fox.append(dict(
            w_in=bf(jnp.concatenate(
                [wf[:, :FOX_W], _pad_last(wf[:, 4 * FOX_W:], LANES), wf[:, FOX_W:2 * FOX_W], zcol,
                 wf[:, 2 * FOX_W:3 * FOX_W], zcol, wf[:, 3 * FOX_W:4 * FOX_W], zcol], axis=1)),
            gq2=jnp.tile(fox_gq[j], 2).reshape(1, LANES), gk2=jnp.tile(fox_gk[j], 2).reshape(1, LANES),
            bf_pad=_pad_last(fox_b_f[j], LANES).reshape(1, LANES),
            w_out=bf(fox_w_out[j])))
    w = dict(norm_mix=norm_mix, norm_ffn=norm_ffn, norm_pe=norm_pe, norm_final=norm_final,
             ret_w_in=bf(ret_w_in), ret_gn=ret_gn, ret_w_out=bf(ret_w_out), mla=mla, fox=fox,
             ffn_w_gate=bf(ffn_w_gate), ffn_w_up=bf(ffn_w_up), ffn_w_down=bf(ffn_w_down),
             pe_w_proj=bf(pe_w_proj), pe_w_gate=bf(pe_w_gate))
    bp = x_prompt.shape[0]
    dt = x_prompt.dtype
    y_p, ret_p, lat_p, kr_p, fk_p, fv_p, flf_p = _trunk(
        x_prompt, p_prompt, 0,
        jnp.zeros((n_ret, bp, RET_HEADS, RET_DK, RET_DV), dt),
        jnp.zeros((n_mla, bp, 0, MLA_KV_LORA), dt), jnp.zeros((n_mla, bp, 0, MLA_ROPE), dt),
        jnp.zeros((n_fox, bp, 0, FOX_HEADS, FOX_DH), dt), jnp.zeros((n_fox, bp, 0, FOX_HEADS, FOX_DH), dt),
        jnp.zeros((n_fox, bp, 0, FOX_HEADS), dt), w)
    y_s, ret_s, lat_s, kr_s, fk_s, fv_s, flf_s = _trunk(
        x_sample, p_sample, cache_mla_latent.shape[2], state_ret, cache_mla_latent, cache_mla_krope,
        cache_fox_k, cache_fox_v, cache_fox_logf, w)
    return (y_p, y_s, ret_p, ret_s, lat_p, kr_p, lat_s, kr_s, fk_p, fv_p, flf_p, fk_s, fv_s, flf_s)
```

```python
import functools
import math

import jax
import jax.numpy as jnp
from jax import lax
from jax.experimental import pallas as pl
from jax.experimental.pallas import tpu as pltpu

D_MODEL = 1024
DEPTH = 4
CHUNK = 64
PE_DIM = 256
N_MIXERS = 3
EPS = 1e-6
ROPE_THETA = 10000.0
RET_HEADS = 4
RET_DK = 256
RET_DV = 512
RET_QK = RET_HEADS * RET_DK
RET_VW = RET_HEADS * RET_DV
MLA_HEADS = 16
MLA_NOPE = 128
MLA_ROPE = 64
MLA_V = 128
MLA_Q_LORA = 512
MLA_KV_LORA = 256
FOX_HEADS = 16
FOX_DH = 64
FOX_W = FOX_HEADS * FOX_DH
D_FF = ((8 * D_MODEL + 3 * 256 - 1) // (3 * 256)) * 256

LANES = 128
MLA_DK_PAD = 2 * LANES
MLA_IN_PAD = 7 * LANES
PROJ_TN = 512
ATTN_NSUB = 2
FOX_SEG = FOX_W + LANES
MASK_VALUE = -1e30
LOG2E = math.log2(math.e)
VMEM_LIMIT = 48 * 1024 * 1024
VMEM_LIMIT_BIG = 56 * 1024 * 1024

F32 = jnp.float32
BF16 = jnp.bfloat16
NT_DIMS = (((1,), (1,)), ((), ()))


def _cparams(*sem, vmem=VMEM_LIMIT):
    return pltpu.CompilerParams(dimension_semantics=sem, vmem_limit_bytes=vmem)


def _pick(n, prefs):
    for p in prefs:
        if n % p == 0:
            return p
    return n


def _rms_bf16(x, g):
    ms = jnp.mean(x * x, axis=-1, keepdims=True)
    return (x * lax.rsqrt(ms + EPS) * g).astype(BF16)


def _pos_table_spec(t, tm):
    if t % tm == 0:
        nt = t // tm
        return pl.BlockSpec((tm, LANES), lambda i, *_: (i % nt, 0))
    return pl.BlockSpec((tm, LANES), lambda i, *_: (0, 0))


def _pos_table(tab, t, tm):
    return tab if t % tm == 0 else jnp.tile(tab, (tm // t, 1))


def _norm_mm_kernel(x_ref, g_ref, w_ref, o_ref, xn_ref):
    @pl.when(pl.program_id(1) == 0)
    def _():
        xn_ref[...] = _rms_bf16(x_ref[...], g_ref[...])

    o_ref[...] = jnp.dot(xn_ref[...], w_ref[...], preferred_element_type=F32).astype(o_ref.dtype)


def norm_matmul(x, g, w, out_dtype=F32, name="norm_mm"):
    m, k = x.shape
    n = w.shape[1]
    tm = _pick(m, (1024, 512, 256))
    tn = _pick(n, (1024, 768, 512, 896, 128))
    return pl.pallas_call(
        _norm_mm_kernel,
        grid=(m // tm, n // tn),
        in_specs=[
            pl.BlockSpec((tm, k), lambda i, j: (i, 0)),
            pl.BlockSpec((1, k), lambda i, j: (0, 0)),
            pl.BlockSpec((k, tn), lambda i, j: (0, j)),
        ],
        out_specs=pl.BlockSpec((tm, tn), lambda i, j: (i, j)),
        out_shape=jax.ShapeDtypeStruct((m, n), out_dtype),
        scratch_shapes=[pltpu.VMEM((tm, k), BF16)],
        compiler_params=_cparams("parallel", "arbitrary"),
        name=name,
    )(x, g.reshape(1, k), w)


def _mm_res_kernel(a_ref, w_ref, r_ref, o_ref):
    o_ref[...] = r_ref[...] + jnp.dot(a_ref[...], w_ref[...], preferred_element_type=F32)


def matmul_res(a, w, res, name="mm"):
    m, k = a.shape
    n = w.shape[1]
    tm = _pick(m, (1024, 512, 256))
    tn = _pick(n, (1024, 512, 128))
    return pl.pallas_call(
        _mm_res_kernel,
        grid=(m // tm, n // tn),
        in_specs=[
            pl.BlockSpec((tm, k), lambda i, j: (i, 0)),
            pl.BlockSpec((k, tn), lambda i, j: (0, j)),
            pl.BlockSpec((tm, tn), lambda i, j: (i, j)),
        ],
        out_specs=pl.BlockSpec((tm, tn), lambda i, j: (i, j)),
        out_shape=jax.ShapeDtypeStruct((m, n), F32),
        compiler_params=_cparams("parallel", "arbitrary"),
        name=name,
    )(a, w, res)


def _ffn_pe_kernel(x_ref, p_ref, g_ref, wg_ref, wu_ref, wd_ref, gp_ref, wpg_ref, wpp_ref, *rest, final):
    o_ref = rest[-1]
    x = x_ref[...]
    xn = _rms_bf16(x, g_ref[...])
    o_ref[...] = x
    dff = wg_ref.shape[1]
    for lo in range(0, dff, PROJ_TN):
        wd_rows = min(PROJ_TN, dff - lo)
        gate = jnp.dot(xn, wg_ref[:, lo:lo + wd_rows], preferred_element_type=F32)
        up = jnp.dot(xn, wu_ref[:, lo:lo + wd_rows], preferred_element_type=F32)
        act = (gate * jax.nn.sigmoid(gate) * up).astype(BF16)
        o_ref[...] += jnp.dot(act, wd_ref[lo:lo + wd_rows, :], preferred_element_type=F32)

    h = o_ref[...]
    hn = _rms_bf16(h, gp_ref[...])
    pb = p_ref[...].astype(BF16)
    for lo in range(0, h.shape[1], PROJ_TN):
        gate = jax.nn.sigmoid(jnp.dot(hn, wpg_ref[:, lo:lo + PROJ_TN], preferred_element_type=F32))
        proj = jnp.dot(pb, wpp_ref[:, lo:lo + PROJ_TN], preferred_element_type=F32)
        o_ref[:, lo:lo + PROJ_TN] = h[:, lo:lo + PROJ_TN] + gate * proj
    if final:
        out = o_ref[...]
        ms = jnp.mean(out * out, axis=-1, keepdims=True)
        o_ref[...] = out * lax.rsqrt(ms + EPS) * rest[0][...]


def ffn_pe(h, p, layer, g, wg, wu, wd, gp, wpg, wpp, final_gain=None):
    m, d = h.shape
    dff = wg.shape[1]
    pd = p.shape[2]
    tm = _pick(m, (512, 256))
    once = pl.Buffered(1)
    row = pl.BlockSpec((1, d), lambda i: (0, 0), pipeline_mode=once)
    in_specs = [
        pl.BlockSpec((tm, d), lambda i: (i, 0)),
        pl.BlockSpec((None, tm, pd), lambda i: (layer, i, 0)),
        row,
        pl.BlockSpec((d, dff), lambda i: (0, 0), pipeline_mode=once),
        pl.BlockSpec((d, dff), lambda i: (0, 0), pipeline_mode=once),
        pl.BlockSpec((dff, d), lambda i: (0, 0), pipeline_mode=once),
        row,
        pl.BlockSpec((d, d), lambda i: (0, 0), pipeline_mode=once),
        pl.BlockSpec((pd, d), lambda i: (0, 0), pipeline_mode=once),
    ]
    args = [h, p, g.reshape(1, d), wg, wu, wd, gp.reshape(1, d), wpg, wpp]
    if final_gain is not None:
        in_specs.append(row)
        args.append(final_gain.reshape(1, d))
    return pl.pallas_call(
        functools.partial(_ffn_pe_kernel, final=final_gain is not None),
        grid=(m // tm,),
        in_specs=in_specs,
        out_specs=pl.BlockSpec((tm, d), lambda i: (i, 0)),
        out_shape=jax.ShapeDtypeStruct((m, d), F32),
        compiler_params=_cparams("parallel", vmem=VMEM_LIMIT_BIG),
        name="ffn_pe",
    )(*args)


def _attn_kernel(*refs, tq, nq, tkp, mw, nsub, hpb, has_bias, has_gate):
    q_ref, k_ref, v_ref, mb_ref = refs[:4]
    pos = 4
    if has_bias:
        fq_ref, fk_ref = refs[4:6]
        pos = 6
    if has_gate:
        g_ref = refs[pos]
        pos += 1
    o_ref = refs[pos]
    bw_k = q_ref.shape[2] // nsub
    bw_v = v_ref.shape[2] // nsub
    mb = mb_ref[...]
    lane_k = lax.broadcasted_iota(jnp.int32, (tq, bw_k), 1)
    lane_v = lax.broadcasted_iota(jnp.int32, (tq, bw_v), 1)

    for qi in reversed(range(nq)):
        r0 = qi * tq
        kv_len = tkp - (nq - 1 - qi) * tq
        head_len = kv_len - mw
        for sb in range(nsub):
            kc = slice(sb * bw_k, (sb + 1) * bw_k)
            vc = slice(sb * bw_v, (sb + 1) * bw_v)
            q_blk = q_ref[0, r0:r0 + tq, kc]
            out = None
            for hh in range(hpb):
                q = q_blk if hpb == 1 else jnp.where(lane_k // (bw_k // hpb) == hh, q_blk, jnp.zeros_like(q_blk))
                hb = sb * hpb + hh

                def scores(lo, hi):
                    s = lax.dot_general(q, k_ref[0, lo:hi, kc], NT_DIMS, preferred_element_type=F32)
                    if has_bias:
                        s = s - fk_ref[0, hb, :, lo:hi]
                    return s

                s_tail = scores(head_len, kv_len) + mb
                m = jnp.max(s_tail, axis=1, keepdims=True)
                if head_len > 0:
                    s_head = scores(0, head_len)
                    m = jnp.maximum(m, jnp.max(s_head, axis=1, keepdims=True))
                if has_bias:
                    head = pl.program_id(1) * (nsub * hpb) + hb
                    f_rows = fq_ref[0, r0:r0 + tq, :]
                    lane_f = lax.broadcasted_iota(jnp.int32, f_rows.shape, 1)
                    fq = jnp.sum(jnp.where(lane_f == head, f_rows, 0.0), axis=1, keepdims=True)
                    shift = fq - (fq + m)
                else:
                    shift = -m
                p_tail = jnp.exp2(s_tail + shift)
                l = jnp.sum(p_tail, axis=1, keepdims=True)
                pv = jnp.dot(p_tail.astype(BF16), v_ref[0, head_len:kv_len, vc], preferred_element_type=F32)
                if head_len > 0:
                    p_head = jnp.exp2(s_head + shift)
                    l = l + jnp.sum(p_head, axis=1, keepdims=True)
                    pv = pv + jnp.dot(p_head.astype(BF16), v_ref[0, 0:head_len, vc], preferred_element_type=F32)
                o_h = pv * (1.0 / l)
                out = o_h if hh == 0 else jnp.where(lane_v // (bw_v // hpb) == hh, o_h, out)
            if has_gate:
                out = jax.nn.sigmoid(g_ref[0, r0:r0 + tq, vc].astype(F32)) * out
            o_ref[0, r0:r0 + tq, vc] = out.astype(o_ref.dtype)


def attention(q, k, v, *, mode, n_blocks, nsub, hpb, tk_real, name, fq=None, fk=None, gate=None):
    b, tq_all, wq = q.shape
    tkp = k.shape[1]
    bw_k = wq // n_blocks
    bw_v = v.shape[2] // n_blocks
    tq = _pick(tq_all, (512,))
    nq = tq_all // tq
    mw = tq if nq > 1 or tq_all == tkp else LANES
    q0 = tk_real - tq
    k0 = tkp - mw
    qpos = q0 + lax.broadcasted_iota(jnp.int32, (tq, mw), 0)
    kpos = k0 + lax.broadcasted_iota(jnp.int32, (tq, mw), 1)
    allowed = (kpos // CHUNK <= qpos // CHUNK) if mode == "chunk" else (kpos <= qpos)
    allowed = jnp.logical_and(allowed, kpos < tk_real)
    mb = jnp.where(allowed, 0.0, MASK_VALUE).astype(F32)

    in_specs = [
        pl.BlockSpec((1, tq_all, bw_k), lambda bi, hi: (bi, 0, hi)),
        pl.BlockSpec((1, tkp, bw_k), lambda bi, hi: (bi, 0, hi)),
        pl.BlockSpec((1, tkp, bw_v), lambda bi, hi: (bi, 0, hi)),
        pl.BlockSpec((tq, mw), lambda bi, hi: (0, 0)),
    ]
    args = [q, k, v, mb]
    if fq is not None:
        in_specs += [
            pl.BlockSpec((1, tq_all, LANES), lambda bi, hi: (bi, 0, 0)),
            pl.BlockSpec((1, nsub * hpb, 1, tkp), lambda bi, hi: (bi, hi, 0, 0)),
        ]
        args += [fq, fk]
    if gate is not None:
        in_specs.append(pl.BlockSpec((1, tq_all, bw_v), lambda bi, hi: (bi, 0, hi)))
        args.append(gate)
    kern = functools.partial(_attn_kernel, tq=tq, nq=nq, tkp=tkp, mw=mw, nsub=nsub, hpb=hpb,
                             has_bias=fq is not None, has_gate=gate is not None)
    return pl.pallas_call(
        kern,
        grid=(b, n_blocks),
        in_specs=in_specs,
        out_specs=pl.BlockSpec((1, tq_all, bw_v), lambda bi, hi: (bi, 0, hi)),
        out_shape=jax.ShapeDtypeStruct((b, tq_all, n_blocks * bw_v), BF16),
        compiler_params=_cparams("parallel", "parallel"),
        name=name,
    )(*args)


def _ret_in_kernel(x_ref, g_ref, w_ref, cos_ref, sin_ref, q_ref, k_ref, v_ref, gate_ref, xn_ref):
    j = pl.program_id(1)
    nv = RET_VW // RET_QK

    @pl.when(j == 0)
    def _():
        xn_ref[...] = _rms_bf16(x_ref[...], g_ref[...])

    def chunks():
        for lo in range(0, RET_QK, PROJ_TN):
            yield lo, jnp.dot(xn_ref[...], w_ref[:, lo:lo + PROJ_TN], preferred_element_type=F32)

    def rope_store(dst, mult):
        cos = cos_ref[...]
        sin = sin_ref[...]
        half = RET_DK // 2
        for lo, acc in chunks():
            for hh in range(PROJ_TN // RET_DK):
                c0 = hh * RET_DK
                x1 = acc[:, c0:c0 + half]
                x2 = acc[:, c0 + half:c0 + RET_DK]
                dst[:, lo + c0:lo + c0 + half] = ((x1 * cos - x2 * sin) * mult).astype(BF16)
                dst[:, lo + c0 + half:lo + c0 + RET_DK] = ((x2 * cos + x1 * sin) * mult).astype(BF16)

    def plain_store(dst):
        for lo, acc in chunks():
            dst[:, lo:lo + PROJ_TN] = acc.astype(BF16)

    @pl.when(j == 0)
    def _():
        rope_store(q_ref, 1.0)

    @pl.when(j == 1)
    def _():
        rope_store(k_ref, RET_DK ** -0.5)

    @pl.when(jnp.logical_and(j >= 2, j < 2 + nv))
    def _():
        plain_store(v_ref)

    @pl.when(j >= 2 + nv)
    def _():
        plain_store(gate_ref)


def ret_in(x, g, w, cos, sin, t):
    m, d = x.shape
    tm = _pick(m, (1024, 512))
    nv = RET_VW // RET_QK
    nj = 2 + 2 * nv
    clip = lambda j, lo, n: jnp.clip(j - lo, 0, n - 1)
    tab = _pos_table_spec(t, tm)
    return pl.pallas_call(
        _ret_in_kernel,
        grid=(m // tm, nj),
        in_specs=[
            pl.BlockSpec((tm, d), lambda i, j: (i, 0)),
            pl.BlockSpec((1, d), lambda i, j: (0, 0)),
            pl.BlockSpec((d, RET_QK), lambda i, j: (0, j)),
            tab, tab,
        ],
        out_specs=[
            pl.BlockSpec((tm, RET_QK), lambda i, j: (i, 0)),
            pl.BlockSpec((tm, RET_QK), lambda i, j: (i, 0)),
            pl.BlockSpec((tm, RET_QK), lambda i, j: (i, clip(j, 2, nv))),
            pl.BlockSpec((tm, RET_QK), lambda i, j: (i, clip(j, 2 + nv, nv))),
        ],
        out_shape=[
            jax.ShapeDtypeStruct((m, RET_QK), BF16),
            jax.ShapeDtypeStruct((m, RET_QK), BF16),
            jax.ShapeDtypeStruct((m, RET_VW), BF16),
            jax.ShapeDtypeStruct((m, RET_VW), BF16),
        ],
        scratch_shapes=[pltpu.VMEM((tm, d), BF16)],
        compiler_params=_cparams("parallel", "arbitrary", vmem=VMEM_LIMIT_BIG),
        name="ret_in",
    )(x, g.reshape(1, d), w, _pos_table(cos, t, tm), _pos_table(sin, t, tm))


def _retention_kernel(lg_ref, q_ref, k_ref, v_ref, gate_ref, gn_ref, s0_ref, a_ref, s_ref, *, chunk, nc):
    lg = lg_ref[pl.program_id(1)]
    row = lax.broadcasted_iota(jnp.int32, (chunk, chunk), 0)
    col = lax.broadcasted_iota(jnp.int32, (chunk, chunk), 1)
    diff = row - col
    dmask = jnp.where(diff >= 0, jnp.exp(lg * jnp.maximum(diff, 0).astype(F32)), 0.0)
    idx = lax.broadcasted_iota(jnp.int32, (chunk, 1), 0).astype(F32)
    q_dec = jnp.exp(lg * (idx + 1.0))
    k_dec = jnp.exp(lg * (chunk - 1.0 - idx))
    c_dec = jnp.exp(lg * jnp.full((1, RET_DV), float(chunk), F32))
    gn = gn_ref[...]

    state = s0_ref[0, 0]
    for c in range(nc):
        rows = slice(c * chunk, (c + 1) * chunk)
        q = q_ref[0, rows, :]
        k = k_ref[0, rows, :]
        v = v_ref[0, rows, :]
        att = lax.dot_general(q, k, NT_DIMS, preferred_element_type=F32) * dmask
        qd = (q.astype(F32) * q_dec).astype(BF16)
        o = jnp.dot(att.astype(BF16), v, preferred_element_type=F32)
        o = o + jnp.dot(qd, state.astype(BF16), preferred_element_type=F32)
        kd = (k.astype(F32) * k_dec).astype(BF16)
        upd = lax.dot_general(kd, v, (((0,), (0,)), ((), ())), preferred_element_type=F32)
        state = state * c_dec + upd

        mu = jnp.mean(o, axis=-1, keepdims=True)
        oc = o - mu
        var = jnp.mean(oc * oc, axis=-1, keepdims=True)
        y = oc * lax.rsqrt(var + EPS) * gn
        g = gate_ref[0, rows, :].astype(F32)
        a_ref[0, rows, :] = (g * jax.nn.sigmoid(g) * y).astype(BF16)
    s_ref[0, 0] = state


def retention(q, k, v, gate, gn, s0, lg, chunk):
    b, t, _ = q.shape
    nc = t // chunk
    kern = functools.partial(_retention_kernel, chunk=chunk, nc=nc)
    tok = lambda bi, hi, lg: (bi, 0, hi)
    st = lambda bi, hi, lg: (bi, hi, 0, 0)
    return pl.pallas_call(
        kern,
        grid_spec=pltpu.PrefetchScalarGridSpec(
            num_scalar_prefetch=1,
            grid=(b, RET_HEADS),
            in_specs=[
                pl.BlockSpec((1, t, RET_DK), tok),
                pl.BlockSpec((1, t, RET_DK), tok),
                pl.BlockSpec((1, t, RET_DV), tok),
                pl.BlockSpec((1, t, RET_DV), tok),
                pl.BlockSpec((1, RET_DV), lambda bi, hi, lg: (0, hi)),
                pl.BlockSpec((1, 1, RET_DK, RET_DV), st),
            ],
            out_specs=[
                pl.BlockSpec((1, t, RET_DV), tok),
                pl.BlockSpec((1, 1, RET_DK, RET_DV), st),
            ],
        ),
        out_shape=[
            jax.ShapeDtypeStruct((b, t, RET_VW), BF16),
            jax.ShapeDtypeStruct((b, RET_HEADS, RET_DK, RET_DV), F32),
        ],
        compiler_params=_cparams("parallel", "parallel"),
        name="retention",
    )(lg, q, k, v, gate, gn, s0)


def _rope_lanes(x, c_tab, s1_tab, s2_tab):
    return x * c_tab + pltpu.roll(x, 3 * LANES // 4, 1) * s1_tab + pltpu.roll(x, LANES // 4, 1) * s2_tab


def _mla_q_kernel(p_ref, qn_ref, kvn_ref, gqn_ref, gqr_ref, gkr_ref, w_ref, c_ref, s1_ref, s2_ref,
                  q_ref, lat_ref, kr_ref):
    c_tab, s1_tab, s2_tab = c_ref[...], s1_ref[...], s2_ref[...]

    ckv = p_ref[:, MLA_Q_LORA:MLA_Q_LORA + MLA_KV_LORA]
    lat_ref[...] = ckv * lax.rsqrt(jnp.mean(ckv * ckv, axis=-1, keepdims=True) + EPS) * kvn_ref[...]

    krp = p_ref[:, MLA_Q_LORA + MLA_KV_LORA:MLA_IN_PAD]
    krn = krp * lax.rsqrt(jnp.sum(krp * krp, axis=-1, keepdims=True) * (1.0 / MLA_ROPE) + EPS) * gkr_ref[...]
    kr_ref[...] = _rope_lanes(krn, c_tab, s1_tab, s2_tab)

    cqn = _rms_bf16(p_ref[:, 0:MLA_Q_LORA], qn_ref[...])
    hg = 4
    for g0 in range(0, MLA_HEADS, hg):
        acc = jnp.dot(cqn, w_ref[:, g0 * MLA_DK_PAD:(g0 + hg) * MLA_DK_PAD], preferred_element_type=F32)
        for hh in range(hg):
            a = acc[:, hh * MLA_DK_PAD:hh * MLA_DK_PAD + LANES]
            r = acc[:, hh * MLA_DK_PAD + LANES:(hh + 1) * MLA_DK_PAD]
            an = a * lax.rsqrt(jnp.mean(a * a, axis=-1, keepdims=True) + EPS) * gqn_ref[...]
            rn = r * lax.rsqrt(jnp.sum(r * r, axis=-1, keepdims=True) * (1.0 / MLA_ROPE) + EPS) * gqr_ref[...]
            col = (g0 + hh) * MLA_DK_PAD
            q_ref[:, col:col + LANES] = an.astype(BF16)
            q_ref[:, col + LANES:col + MLA_DK_PAD] = _rope_lanes(rn, c_tab, s1_tab, s2_tab).astype(BF16)


def mla_q(proj, q_norm, kv_norm, gq_nope, gq_rope_pad, gk_rope_pad, w_qb_pad, tabs, t):
    m = proj.shape[0]
    tm = _pick(m, (512, 256))
    row = lambda n: pl.BlockSpec((1, n), lambda i: (0, 0))
    tab = _pos_table_spec(t, tm)
    return pl.pallas_call(
        _mla_q_kernel,
        grid=(m // tm,),
        in_specs=[
            pl.BlockSpec((tm, MLA_IN_PAD), lambda i: (i, 0)),
            row(MLA_Q_LORA), row(MLA_KV_LORA), row(LANES), row(LANES), row(LANES),
            pl.BlockSpec((MLA_Q_LORA, MLA_HEADS * MLA_DK_PAD), lambda i: (0, 0)),
            tab, tab, tab,
        ],
        out_specs=[
            pl.BlockSpec((tm, MLA_HEADS * MLA_DK_PAD), lambda i: (i, 0)),
            pl.BlockSpec((tm, MLA_KV_LORA), lambda i: (i, 0)),
            pl.BlockSpec((tm, LANES), lambda i: (i, 0)),
        ],
        out_shape=[
            jax.ShapeDtypeStruct((m, MLA_HEADS * MLA_DK_PAD), BF16),
            jax.ShapeDtypeStruct((m, MLA_KV_LORA), F32),
            jax.ShapeDtypeStruct((m, LANES), F32),
        ],
        compiler_params=_cparams("parallel"),
        name="mla_q",
    )(proj, q_norm.reshape(1, -1), kv_norm.reshape(1, -1), gq_nope.reshape(1, -1), gq_rope_pad, gk_rope_pad,
      w_qb_pad, *[_pos_table(x, t, tm) for x in tabs])


def _mla_kv_kernel(lat_ref, kr_ref, gk_ref, wk_ref, wv_ref, k_ref, v_ref):
    latb = lat_ref[...].astype(BF16)
    krb = kr_ref[...].astype(BF16)
    v_ref[...] = jnp.dot(latb, wv_ref[...], preferred_element_type=F32).astype(BF16)
    hg = 4
    for g0 in range(0, MLA_HEADS, hg):
        acc = jnp.dot(latb, wk_ref[:, g0 * MLA_NOPE:(g0 + hg) * MLA_NOPE], preferred_element_type=F32)
        for hh in range(hg):
            a = acc[:, hh * MLA_NOPE:(hh + 1) * MLA_NOPE]
            an = a * lax.rsqrt(jnp.mean(a * a, axis=-1, keepdims=True) + EPS) * gk_ref[...]
            col = (g0 + hh) * MLA_DK_PAD
            k_ref[:, col:col + LANES] = an.astype(BF16)
            k_ref[:, col + LANES:col + MLA_DK_PAD] = krb


def mla_kv(lat_all, kr_all, gk_nope, wk, wv):
    m = lat_all.shape[0]
    tm = _pick(m, (512, 256))
    return pl.pallas_call(
        _mla_kv_kernel,
        grid=(m // tm,),
        in_specs=[
            pl.BlockSpec((tm, MLA_KV_LORA), lambda i: (i, 0)),
            pl.BlockSpec((tm, LANES), lambda i: (i, 0)),
            pl.BlockSpec((1, MLA_NOPE), lambda i: (0, 0)),
            pl.BlockSpec((MLA_KV_LORA, MLA_HEADS * MLA_NOPE), lambda i: (0, 0)),
            pl.BlockSpec((MLA_KV_LORA, MLA_HEADS * MLA_V), lambda i: (0, 0)),
        ],
        out_specs=[
            pl.BlockSpec((tm, MLA_HEADS * MLA_DK_PAD), lambda i: (i, 0)),
            pl.BlockSpec((tm, MLA_HEADS * MLA_V), lambda i: (i, 0)),
        ],
        out_shape=[
            jax.ShapeDtypeStruct((m, MLA_HEADS * MLA_DK_PAD), BF16),
            jax.ShapeDtypeStruct((m, MLA_HEADS * MLA_V), BF16),
        ],
        compiler_params=_cparams("parallel"),
        name="mla_kv",
    )(lat_all, kr_all, gk_nope.reshape(1, -1), wk, wv)


def _head_norm(x, gain):
    lane = lax.broadcasted_iota(jnp.int32, (x.shape[0], LANES), 1)
    low = lane < FOX_DH
    outs = []
    for c in range(x.shape[1] // LANES):
        xb = x[:, c * LANES:(c + 1) * LANES]
        sq = xb * xb
        s_lo = jnp.sum(jnp.where(low, sq, 0.0), axis=-1, keepdims=True)
        s_hi = jnp.sum(jnp.where(low, 0.0, sq), axis=-1, keepdims=True)
        ms = jnp.where(low, s_lo, s_hi) * (1.0 / FOX_DH)
        outs.append(xb * lax.rsqrt(ms + EPS) * gain)
    return outs


def _fox_in_kernel(x_ref, g_ref, w_ref, gq_ref, gk_ref, bf_ref,
                   q_ref, k_ref, kb_ref, v_ref, vb_ref, gate_ref, lf_ref, xn_ref):
    j = pl.program_id(1)

    @pl.when(j == 0)
    def _():
        xn_ref[...] = _rms_bf16(x_ref[...], g_ref[...])

    def chunks():
        for lo in range(0, FOX_W, PROJ_TN):
            yield lo, jnp.dot(xn_ref[...], w_ref[:, lo:lo + PROJ_TN], preferred_element_type=F32)

    @pl.when(j == 0)
    def _():
        for lo, acc in chunks():
            for c, blk in enumerate(_head_norm(acc, gq_ref[...])):
                q_ref[:, lo + c * LANES:lo + (c + 1) * LANES] = (blk * (FOX_DH ** -0.5 * LOG2E)).astype(BF16)
        fl = jnp.dot(xn_ref[...], w_ref[:, FOX_W:FOX_SEG], preferred_element_type=F32)
        lf_ref[...] = jax.nn.log_sigmoid(fl + bf_ref[...])

    @pl.when(j == 1)
    def _():
        for lo, acc in chunks():
            for c, blk in enumerate(_head_norm(acc, gk_ref[...])):
                k_ref[:, lo + c * LANES:lo + (c + 1) * LANES] = blk
                kb_ref[:, lo + c * LANES:lo + (c + 1) * LANES] = blk.astype(BF16)

    @pl.when(j == 2)
    def _():
        for lo, acc in chunks():
            v_ref[:, lo:lo + PROJ_TN] = acc
            vb_ref[:, lo:lo + PROJ_TN] = acc.astype(BF16)

    @pl.when(j == 3)
    def _():
        for lo, acc in chunks():
            gate_ref[:, lo:lo + PROJ_TN] = acc.astype(BF16)


def fox_in(x, g, w, gq2, gk2, bf_pad):
    m, d = x.shape
    tm = _pick(m, (512, 256))
    seg = lambda: pl.BlockSpec((tm, FOX_W), lambda i, j: (i, 0))
    row = pl.BlockSpec((1, LANES), lambda i, j: (0, 0))
    return pl.pallas_call(
        _fox_in_kernel,
        grid=(m // tm, 4),
        in_specs=[
            pl.BlockSpec((tm, d), lambda i, j: (i, 0)),
            pl.BlockSpec((1, d), lambda i, j: (0, 0)),
            pl.BlockSpec((d, FOX_SEG), lambda i, j: (0, j)),
            row, row, row,
        ],
        out_specs=[seg(), seg(), seg(), seg(), seg(), seg(),
                   pl.BlockSpec((tm, LANES), lambda i, j: (i, 0))],
        out_shape=[
            jax.ShapeDtypeStruct((m, FOX_W), BF16),
            jax.ShapeDtypeStruct((m, FOX_W), F32),
            jax.ShapeDtypeStruct((m, FOX_W), BF16),
            jax.ShapeDtypeStruct((m, FOX_W), F32),
            jax.ShapeDtypeStruct((m, FOX_W), BF16),
            jax.ShapeDtypeStruct((m, FOX_W), BF16),
            jax.ShapeDtypeStruct((m, LANES), F32),
        ],
        scratch_shapes=[pltpu.VMEM((tm, d), BF16)],
        compiler_params=_cparams("parallel", "arbitrary"),
        name="fox_in",
    )(x, g.reshape(1, d), w, gq2, gk2, bf_pad)


def _fox_cumsum_kernel(lf_ref, ft_ref, fh_ref):
    tkp = lf_ref.shape[1]
    x = lf_ref[0].T[0:FOX_HEADS, :]
    lane = lax.broadcasted_iota(jnp.int32, x.shape, 1)
    shift = 1
    while shift < tkp:
        x = x + jnp.where(lane >= shift, pltpu.roll(x, shift, 1), 0.0)
        shift *= 2
    x = x * LOG2E
    for h in range(FOX_HEADS):
        fh_ref[0, h] = x[h:h + 1, :]
    ft_ref[0] = jnp.concatenate([x, jnp.zeros((LANES - FOX_HEADS, tkp), F32)], axis=0).T


def fox_cumsum(lf_all):
    b, tkp, _ = lf_all.shape
    return pl.pallas_call(
        _fox_cumsum_kernel,
        grid=(b,),
        in_specs=[pl.BlockSpec((1, tkp, LANES), lambda i: (i, 0, 0))],
        out_specs=[pl.BlockSpec((1, tkp, LANES), lambda i: (i, 0, 0)),
                   pl.BlockSpec((1, FOX_HEADS, 1, tkp), lambda i: (i, 0, 0, 0))],
        out_shape=[jax.ShapeDtypeStruct((b, tkp, LANES), F32),
                   jax.ShapeDtypeStruct((b, FOX_HEADS, 1, tkp), F32)],
        compiler_params=_cparams("parallel"),
        name="fox_cumsum",
    )(lf_all)


def _rope_tables(pos, half):
    inv = ROPE_THETA ** (-jnp.arange(half, dtype=F32) / half)
    ang = pos.astype(F32)[:, None] * inv[None, :]
    return jnp.cos(ang), jnp.sin(ang)


def _pad_rows(a, rows):
    pad = rows - a.shape[1]
    if pad == 0:
        return a
    return jnp.pad(a, ((0, 0), (0, pad)) + ((0, 0),) * (a.ndim - 2))


def _retention_mixer(h2, b, t, pos, s0, gmix, w_in, gn, w_out):
    cos, sin = _rope_tables(pos, RET_DK // 2)
    q, k, v, gate = ret_in(h2, gmix, w_in, cos, sin, t)
    lg = jnp.log1p(-jnp.exp2(-5.0 - jnp.arange(RET_HEADS, dtype=F32)))
    chunk = _pick(t, (256, 128, 64))
    a, s = retention(q.reshape(b, t, RET_QK), k.reshape(b, t, RET_QK), v.reshape(b, t, RET_VW),
                     gate.reshape(b, t, RET_VW), gn.reshape(1, RET_VW), s0, lg, chunk)
    return matmul_res(a.reshape(b * t, RET_VW), w_out, h2, name="ret_out"), s


def _mla_mixer(h2, b, t, pos, lat_past, kr_past, gmix, w):
    proj = norm_matmul(h2, gmix, w['w_in'], name="mla_in")
    cos, sin = _rope_tables(pos, MLA_ROPE // 2)
    z = jnp.zeros_like(cos)
    tabs = (jnp.concatenate([cos, cos, z, z], axis=1), jnp.concatenate([-sin, z, z, z], axis=1),
            jnp.concatenate([z, sin, z, z], axis=1))
    q_pad, lat, kr_pad = mla_q(proj, w['q_norm'], w['kv_norm'], w['gq_nope'], w['gq_rope_pad'], w['gk_rope_pad'],
                               w['w_qb_pad'], tabs, t)
    tk = lat_past.shape[1] + t
    tkp = -(-tk // LANES) * LANES
    if lat_past.shape[1] == 0:
        lat_all, kr_all = lat, kr_pad
    else:
        lat_all = _pad_rows(jnp.concatenate([lat_past, lat.reshape(b, t, MLA_KV_LORA)], axis=1), tkp)
        kr_past_pad = jnp.pad(kr_past, ((0, 0), (0, 0), (0, LANES - MLA_ROPE)))
        kr_all = _pad_rows(jnp.concatenate([kr_past_pad, kr_pad.reshape(b, t, LANES)], axis=1), tkp)
        lat_all = lat_all.reshape(b * tkp, MLA_KV_LORA)
        kr_all = kr_all.reshape(b * tkp, LANES)
    k_pad, v = mla_kv(lat_all, kr_all, w['gk_nope'], w['w_kb'], w['w_vb'])
    o = attention(q_pad.reshape(b, t, -1), k_pad.reshape(b, tkp, -1), v.reshape(b, tkp, -1), mode="chunk",
                  n_blocks=MLA_HEADS // ATTN_NSUB, nsub=ATTN_NSUB, hpb=1, tk_real=tk, name="mla_attn")
    h2 = matmul_res(o.reshape(b * t, MLA_HEADS * MLA_V), w['w_out'], h2, name="mla_out")
    return h2, lat.reshape(b, t, MLA_KV_LORA), kr_pad[:, :MLA_ROPE].reshape(b, t, MLA_ROPE)


def _fox_mixer(h2, b, t, k_past, v_past, lf_past, gmix, w):
    q, k, kb, v, vb, gate, lf_pad = fox_in(h2, gmix, w['w_in'], w['gq2'], w['gk2'], w['bf_pad'])
    lf = lf_pad[:, :FOX_HEADS].reshape(b, t, FOX_HEADS)
    tp = k_past.shape[1]
    tk = tp + t
    tkp = -(-tk // LANES) * LANES
    kb = kb.reshape(b, t, FOX_W)
    vb = vb.reshape(b, t, FOX_W)
    if tp > 0:
        kb = _pad_rows(jnp.concatenate([k_past.reshape(b, tp, FOX_W).astype(BF16), kb], axis=1), tkp)
        vb = _pad_rows(jnp.concatenate([v_past.reshape(b, tp, FOX_W).astype(BF16), vb], axis=1), tkp)
    lf_all = lf_pad.reshape(b, t, LANES)
    if tp > 0:
        lf_all = _pad_rows(jnp.concatenate([_pad_last(lf_past.astype(F32), LANES), lf_all], axis=1), tkp)
    f_time, fk = fox_cumsum(lf_all)
    fq = f_time[:, tk - t:tk]
    a = attention(q.reshape(b, t, FOX_W), kb, vb, mode="token", n_blocks=FOX_HEADS // (2 * ATTN_NSUB),
                  nsub=ATTN_NSUB, hpb=2, tk_real=tk, name="fox_attn", fq=fq, fk=fk,
                  gate=gate.reshape(b, t, FOX_W))
    h2 = matmul_res(a.reshape(b * t, FOX_W), w['w_out'], h2, name="fox_out")
    return (h2, k.reshape(b, t, FOX_HEADS, FOX_DH), v.reshape(b, t, FOX_HEADS, FOX_DH), lf)


def _trunk(x, p, past_len, ret_states, lat_past, kr_past, fk_past, fv_past, flf_past, w):
    b, t, d = x.shape
    pos = past_len + jnp.arange(t)
    h2 = x.reshape(b * t, d)
    new_ret, new_lat, new_kr, new_fk, new_fv, new_flf = [], [], [], [], [], []
    for i in range(DEPTH):
        kind, j = i % N_MIXERS, i // N_MIXERS
        if kind == 0:
            h2, s = _retention_mixer(h2, b, t, pos, ret_states[j], w['norm_mix'][i], w['ret_w_in'][j],
                                     w['ret_gn'][j], w['ret_w_out'][j])
            new_ret.append(s)
        elif kind == 1:
            h2, lat, kr = _mla_mixer(h2, b, t, pos, lat_past[j], kr_past[j], w['norm_mix'][i], w['mla'][j])
            new_lat.append(lat)
            new_kr.append(kr)
        else:
            h2, fk, fv, flf = _fox_mixer(h2, b, t, fk_past[j], fv_past[j], flf_past[j], w['norm_mix'][i],
                                         w['fox'][j])
            new_fk.append(fk)
            new_fv.append(fv)
            new_flf.append(flf)
        h2 = ffn_pe(h2, p.reshape(DEPTH, b * t, PE_DIM), i, w['norm_ffn'][i], w['ffn_w_gate'][i],
                    w['ffn_w_up'][i], w['ffn_w_down'][i], w['norm_pe'][i], w['pe_w_gate'][i], w['pe_w_proj'][i],
                    final_gain=w['norm_final'] if i == DEPTH - 1 else None)
    return (h2.reshape(b, t, d), jnp.stack(new_ret), jnp.stack(new_lat), jnp.stack(new_kr),
            jnp.stack(new_fk), jnp.stack(new_fv), jnp.stack(new_flf))


def _pad_last(a, n):
    return jnp.pad(a, [(0, 0)] * (a.ndim - 1) + [(0, n - a.shape[-1])])


def kernel(x_prompt, x_sample, state_ret, cache_mla_latent, cache_mla_krope, cache_fox_k, cache_fox_v,
           cache_fox_logf, p_prompt, p_sample, norm_mix, norm_ffn, norm_pe, norm_final, ret_w_in, ret_gn,
           ret_w_out, mla_w_in, mla_q_norm, mla_kv_norm, mla_w_qb, mla_w_kvb, mla_gq_nope, mla_gq_rope,
           mla_gk_nope, mla_gk_rope, mla_w_out, fox_w_in, fox_b_f, fox_gq, fox_gk, fox_w_out, ffn_w_gate,
           ffn_w_up, ffn_w_down, pe_w_proj, pe_w_gate):
    bf = lambda a: a.astype(BF16)
    n_ret, n_mla, n_fox = state_ret.shape[0], cache_mla_latent.shape[0], cache_fox_k.shape[0]
    mla = []
    qscale = (MLA_NOPE + MLA_ROPE) ** -0.5 * LOG2E
    for j in range(n_mla):
        w_qb = mla_w_qb[j].reshape(MLA_Q_LORA, MLA_HEADS, MLA_NOPE + MLA_ROPE)
        w_kvb = mla_w_kvb[j].reshape(MLA_KV_LORA, MLA_HEADS, MLA_NOPE + MLA_V)
        mla.append(dict(
            w_in=bf(_pad_last(mla_w_in[j], MLA_IN_PAD)),
            q_norm=mla_q_norm[j], kv_norm=mla_kv_norm[j], gk_nope=mla_gk_nope[j],
            gq_nope=mla_gq_nope[j] * qscale,
            gq_rope_pad=_pad_last(mla_gq_rope[j] * qscale, LANES).reshape(1, LANES),
            gk_rope_pad=_pad_last(mla_gk_rope[j], LANES).reshape(1, LANES),
            w_qb_pad=bf(_pad_last(w_qb, MLA_DK_PAD).reshape(MLA_Q_LORA, MLA_HEADS * MLA_DK_PAD)),
            w_kb=bf(w_kvb[:, :, :MLA_NOPE].reshape(MLA_KV_LORA, MLA_HEADS * MLA_NOPE)),
            w_vb=bf(w_kvb[:, :, MLA_NOPE:].reshape(MLA_KV_LORA, MLA_HEADS * MLA_V)),
            w_out=bf(mla_w_out[j])))
    fox = []
    for j in range(n_fox):
        wf = fox_w_in[j]
        zcol = jnp.zeros((D_MODEL, LANES), wf.dtype)
        fox.append(dict(
            w_in=bf(jnp.concatenate(
                [wf[:, :FOX_W], _pad_last(wf[:, 4 * FOX_W:], LANES), wf[:, FOX_W:2 * FOX_W], zcol,
                 wf[:, 2 * FOX_W:3 * FOX_W], zcol, wf[:, 3 * FOX_W:4 * FOX_W], zcol], axis=1)),
            gq2=jnp.tile(fox_gq[j], 2).reshape(1, LANES), gk2=jnp.tile(fox_gk[j], 2).reshape(1, LANES),
            bf_pad=_pad_last(fox_b_f[j], LANES).reshape(1, LANES),
            w_out=bf(fox_w_out[j])))
    w = dict(norm_mix=norm_mix, norm_ffn=norm_ffn, norm_pe=norm_pe, norm_final=norm_final,
             ret_w_in=bf(ret_w_in), ret_gn=ret_gn, ret_w_out=bf(ret_w_out), mla=mla, fox=fox,
             ffn_w_gate=bf(ffn_w_gate), ffn_w_up=bf(ffn_w_up), ffn_w_down=bf(ffn_w_down),
             pe_w_proj=bf(pe_w_proj), pe_w_gate=bf(pe_w_gate))
    bp = x_prompt.shape[0]
    dt = x_prompt.dtype
    y_p, ret_p, lat_p, kr_p, fk_p, fv_p, flf_p = _trunk(
        x_prompt, p_prompt, 0,
        jnp.zeros((n_ret, bp, RET_HEADS, RET_DK, RET_DV), dt),
        jnp.zeros((n_mla, bp, 0, MLA_KV_LORA), dt), jnp.zeros((n_mla, bp, 0, MLA_ROPE), dt),
        jnp.zeros((n_fox, bp, 0, FOX_HEADS, FOX_DH), dt), jnp.zeros((n_fox, bp, 0, FOX_HEADS, FOX_DH), dt),
        jnp.zeros((n_fox, bp, 0, FOX_HEADS), dt), w)
    y_s, ret_s, lat_s, kr_s, fk_s, fv_s, flf_s = _trunk(
        x_sample, p_sample, cache_mla_latent.shape[2], state_ret, cache_mla_latent, cache_mla_krope,
        cache_fox_k, cache_fox_v, cache_fox_logf, w)
    return (y_p, y_s, ret_p, ret_s, lat_p, kr_p, lat_s, kr_s, fk_p, fv_p, flf_p, fk_s, fv_s, flf_s)
```

```python
import functools
import math

import jax
import jax.numpy as jnp
from jax import lax
from jax.experimental import pallas as pl
from jax.experimental.pallas import tpu as pltpu

D_MODEL = 1024
DEPTH = 4
CHUNK = 64
PE_DIM = 256
N_MIXERS = 3
EPS = 1e-6
ROPE_THETA = 10000.0
RET_HEADS = 4
RET_DK = 256
RET_DV = 512
RET_QK = RET_HEADS * RET_DK
RET_VW = RET_HEADS * RET_DV
MLA_HEADS = 16
MLA_NOPE = 128
MLA_ROPE = 64
MLA_V = 128
MLA_Q_LORA = 512
MLA_KV_LORA = 256
FOX_HEADS = 16
FOX_DH = 64
FOX_W = FOX_HEADS * FOX_DH
D_FF = ((8 * D_MODEL + 3 * 256 - 1) // (3 * 256)) * 256

LANES = 128
MLA_DK_PAD = 2 * LANES
MLA_IN_PAD = 7 * LANES
PROJ_TN = 512
ATTN_NSUB = 2
MASK_VALUE = -1e30
LOG2E = math.log2(math.e)
VMEM_LIMIT = 48 * 1024 * 1024
VMEM_LIMIT_BIG = 56 * 1024 * 1024

F32 = jnp.float32
BF16 = jnp.bfloat16
NT_DIMS = (((1,), (1,)), ((), ()))


def _cparams(*sem, vmem=VMEM_LIMIT):
    return pltpu.CompilerParams(dimension_semantics=sem, vmem_limit_bytes=vmem)


def _pick(n, prefs):
    for p in prefs:
        if n % p == 0:
            return p
    return n


def _rms_bf16(x, g):
    ms = jnp.mean(x * x, axis=-1, keepdims=True)
    return (x * lax.rsqrt(ms + EPS) * g).astype(BF16)


def _pos_table_spec(t, tm):
    if t % tm == 0:
        nt = t // tm
        return pl.BlockSpec((tm, LANES), lambda i, *_: (i % nt, 0))
    return pl.BlockSpec((tm, LANES), lambda i, *_: (0, 0))


def _pos_table(tab, t, tm):
    return tab if t % tm == 0 else jnp.tile(tab, (tm // t, 1))


def _norm_mm_kernel(x_ref, g_ref, w_ref, o_ref, xn_ref):
    @pl.when(pl.program_id(1) == 0)
    def _():
        xn_ref[...] = _rms_bf16(x_ref[...], g_ref[...])

    o_ref[...] = jnp.dot(xn_ref[...], w_ref[...], preferred_element_type=F32).astype(o_ref.dtype)


def norm_matmul(x, g, w, out_dtype=F32, name="norm_mm"):
    m, k = x.shape
    n = w.shape[1]
    tm = _pick(m, (1024, 512, 256))
    tn = _pick(n, (1024, 768, 512, 896, 128))
    return pl.pallas_call(
        _norm_mm_kernel,
        grid=(m // tm, n // tn),
        in_specs=[
            pl.BlockSpec((tm, k), lambda i, j: (i, 0)),
            pl.BlockSpec((1, k), lambda i, j: (0, 0)),
            pl.BlockSpec((k, tn), lambda i, j: (0, j)),
        ],
        out_specs=pl.BlockSpec((tm, tn), lambda i, j: (i, j)),
        out_shape=jax.ShapeDtypeStruct((m, n), out_dtype),
        scratch_shapes=[pltpu.VMEM((tm, k), BF16)],
        compiler_params=_cparams("parallel", "arbitrary"),
        name=name,
    )(x, g.reshape(1, k), w)


def _mm_res_kernel(a_ref, w_ref, r_ref, o_ref):
    o_ref[...] = r_ref[...] + jnp.dot(a_ref[...], w_ref[...], preferred_element_type=F32)


def matmul_res(a, w, res, name="mm"):
    m, k = a.shape
    n = w.shape[1]
    tm = _pick(m, (1024, 512, 256))
    tn = _pick(n, (1024, 512, 128))
    return pl.pallas_call(
        _mm_res_kernel,
        grid=(m // tm, n // tn),
        in_specs=[
            pl.BlockSpec((tm, k), lambda i, j: (i, 0)),
            pl.BlockSpec((k, tn), lambda i, j: (0, j)),
            pl.BlockSpec((tm, tn), lambda i, j: (i, j)),
        ],
        out_specs=pl.BlockSpec((tm, tn), lambda i, j: (i, j)),
        out_shape=jax.ShapeDtypeStruct((m, n), F32),
        compiler_params=_cparams("parallel", "arbitrary"),
        name=name,
    )(a, w, res)


def _ffn_pe_kernel(x_ref, p_ref, g_ref, wg_ref, wu_ref, wd_ref, gp_ref, wpg_ref, wpp_ref, *rest, final):
    o_ref = rest[-1]
    x = x_ref[...]
    xn = _rms_bf16(x, g_ref[...])
    o_ref[...] = x
    dff = wg_ref.shape[1]
    for lo in range(0, dff, PROJ_TN):
        wd_rows = min(PROJ_TN, dff - lo)
        gate = jnp.dot(xn, wg_ref[:, lo:lo + wd_rows], preferred_element_type=F32)
        up = jnp.dot(xn, wu_ref[:, lo:lo + wd_rows], preferred_element_type=F32)
        act = (gate * jax.nn.sigmoid(gate) * up).astype(BF16)
        o_ref[...] += jnp.dot(act, wd_ref[lo:lo + wd_rows, :], preferred_element_type=F32)

    h = o_ref[...]
    hn = _rms_bf16(h, gp_ref[...])
    pb = p_ref[...].astype(BF16)
    for lo in range(0, h.shape[1], PROJ_TN):
        gate = jax.nn.sigmoid(jnp.dot(hn, wpg_ref[:, lo:lo + PROJ_TN], preferred_element_type=F32))
        proj = jnp.dot(pb, wpp_ref[:, lo:lo + PROJ_TN], preferred_element_type=F32)
        o_ref[:, lo:lo + PROJ_TN] = h[:, lo:lo + PROJ_TN] + gate * proj
    if final:
        out = o_ref[...]
        ms = jnp.mean(out * out, axis=-1, keepdims=True)
        o_ref[...] = out * lax.rsqrt(ms + EPS) * rest[0][...]


def ffn_pe(h, p, layer, g, wg, wu, wd, gp, wpg, wpp, final_gain=None):
    m, d = h.shape
    dff = wg.shape[1]
    pd = p.shape[2]
    tm = _pick(m, (512, 256))
    once = pl.Buffered(1)
    row = pl.BlockSpec((1, d), lambda i: (0, 0), pipeline_mode=once)
    in_specs = [
        pl.BlockSpec((tm, d), lambda i: (i, 0)),
        pl.BlockSpec((None, tm, pd), lambda i: (layer, i, 0)),
        row,
        pl.BlockSpec((d, dff), lambda i: (0, 0), pipeline_mode=once),
        pl.BlockSpec((d, dff), lambda i: (0, 0), pipeline_mode=once),
        pl.BlockSpec((dff, d), lambda i: (0, 0), pipeline_mode=once),
        row,
        pl.BlockSpec((d, d), lambda i: (0, 0), pipeline_mode=once),
        pl.BlockSpec((pd, d), lambda i: (0, 0), pipeline_mode=once),
    ]
    args = [h, p, g.reshape(1, d), wg, wu, wd, gp.reshape(1, d), wpg, wpp]
    if final_gain is not None:
        in_specs.append(row)
        args.append(final_gain.reshape(1, d))
    return pl.pallas_call(
        functools.partial(_ffn_pe_kernel, final=final_gain is not None),
        grid=(m // tm,),
        in_specs=in_specs,
        out_specs=pl.BlockSpec((tm, d), lambda i: (i, 0)),
        out_shape=jax.ShapeDtypeStruct((m, d), F32),
        compiler_params=_cparams("parallel", vmem=VMEM_LIMIT_BIG),
        name="ffn_pe",
    )(*args)


def _attn_kernel(*refs, tq, nq, tkp, mw, nsub, hpb, has_bias, has_gate):
    q_ref, k_ref, v_ref, mb_ref = refs[:4]
    pos = 4
    if has_bias:
        fq_ref, fk_ref = refs[4:6]
        pos = 6
    if has_gate:
        g_ref = refs[pos]
        pos += 1
    o_ref = refs[pos]
    bw_k = q_ref.shape[2] // nsub
    bw_v = v_ref.shape[2] // nsub
    mb = mb_ref[...]
    lane_k = lax.broadcasted_iota(jnp.int32, (tq, bw_k), 1)
    lane_v = lax.broadcasted_iota(jnp.int32, (tq, bw_v), 1)

    for qi in reversed(range(nq)):
        r0 = qi * tq
        kv_len = tkp - (nq - 1 - qi) * tq
        head_len = kv_len - mw
        for sb in range(nsub):
            kc = slice(sb * bw_k, (sb + 1) * bw_k)
            vc = slice(sb * bw_v, (sb + 1) * bw_v)
            q_blk = q_ref[0, r0:r0 + tq, kc]
            out = None
            for hh in range(hpb):
                q = q_blk if hpb == 1 else jnp.where(lane_k // (bw_k // hpb) == hh, q_blk, jnp.zeros_like(q_blk))
                hb = sb * hpb + hh

                def scores(lo, hi):
                    s = lax.dot_general(q, k_ref[0, lo:hi, kc], NT_DIMS, preferred_element_type=F32)
                    if has_bias:
                        s = s - fk_ref[0, hb, :, lo:hi]
                    return s

                s_tail = scores(head_len, kv_len) + mb
                m = jnp.max(s_tail, axis=1, keepdims=True)
                if head_len > 0:
                    s_head = scores(0, head_len)
                    m = jnp.maximum(m, jnp.max(s_head, axis=1, keepdims=True))
                if has_bias:
                    head = pl.program_id(1) * (nsub * hpb) + hb
                    f_rows = fq_ref[0, r0:r0 + tq, :]
                    lane_f = lax.broadcasted_iota(jnp.int32, f_rows.shape, 1)
                    fq = jnp.sum(jnp.where(lane_f == head, f_rows, 0.0), axis=1, keepdims=True)
                    shift = fq - (fq + m)
                else:
                    shift = -m
                p_tail = jnp.exp2(s_tail + shift)
                l = jnp.sum(p_tail, axis=1, keepdims=True)
                pv = jnp.dot(p_tail.astype(BF16), v_ref[0, head_len:kv_len, vc], preferred_element_type=F32)
                if head_len > 0:
                    p_head = jnp.exp2(s_head + shift)
                    l = l + jnp.sum(p_head, axis=1, keepdims=True)
                    pv = pv + jnp.dot(p_head.astype(BF16), v_ref[0, 0:head_len, vc], preferred_element_type=F32)
                o_h = pv * (1.0 / l)
                out = o_h if hh == 0 else jnp.where(lane_v // (bw_v // hpb) == hh, o_h, out)
            if has_gate:
                out = jax.nn.sigmoid(g_ref[0, r0:r0 + tq, vc].astype(F32)) * out
            o_ref[0, r0:r0 + tq, vc] = out.astype(o_ref.dtype)


def attention(q, k, v, *, mode, n_blocks, nsub, hpb, tk_real, name, fq=None, fk=None, gate=None):
    b, tq_all, wq = q.shape
    tkp = k.shape[1]
    bw_k = wq // n_blocks
    bw_v = v.shape[2] // n_blocks
    tq = _pick(tq_all, (512,))
    nq = tq_all // tq
    mw = tq if nq > 1 or tq_all == tkp else LANES
    q0 = tk_real - tq
    k0 = tkp - mw
    qpos = q0 + lax.broadcasted_iota(jnp.int32, (tq, mw), 0)
    kpos = k0 + lax.broadcasted_iota(jnp.int32, (tq, mw), 1)
    allowed = (kpos // CHUNK <= qpos // CHUNK) if mode == "chunk" else (kpos <= qpos)
    allowed = jnp.logical_and(allowed, kpos < tk_real)
    mb = jnp.where(allowed, 0.0, MASK_VALUE).astype(F32)

    in_specs = [
        pl.BlockSpec((1, tq_all, bw_k), lambda bi, hi: (bi, 0, hi)),
        pl.BlockSpec((1, tkp, bw_k), lambda bi, hi: (bi, 0, hi)),
        pl.BlockSpec((1, tkp, bw_v), lambda bi, hi: (bi, 0, hi)),
        pl.BlockSpec((tq, mw), lambda bi, hi: (0, 0)),
    ]
    args = [q, k, v, mb]
    if fq is not None:
        in_specs += [
            pl.BlockSpec((1, tq_all, LANES), lambda bi, hi: (bi, 0, 0)),
            pl.BlockSpec((1, nsub * hpb, 1, tkp), lambda bi, hi: (bi, hi, 0, 0)),
        ]
        args += [fq, fk]
    if gate is not None:
        in_specs.append(pl.BlockSpec((1, tq_all, bw_v), lambda bi, hi: (bi, 0, hi)))
        args.append(gate)
    kern = functools.partial(_attn_kernel, tq=tq, nq=nq, tkp=tkp, mw=mw, nsub=nsub, hpb=hpb,
                             has_bias=fq is not None, has_gate=gate is not None)
    return pl.pallas_call(
        kern,
        grid=(b, n_blocks),
        in_specs=in_specs,
        out_specs=pl.BlockSpec((1, tq_all, bw_v), lambda bi, hi: (bi, 0, hi)),
        out_shape=jax.ShapeDtypeStruct((b, tq_all, n_blocks * bw_v), BF16),
        compiler_params=_cparams("parallel", "parallel"),
        name=name,
    )(*args)


def _ret_in_kernel(x_ref, g_ref, w_ref, cos_ref, sin_ref, q_ref, k_ref, v_ref, gate_ref):
    xn = _rms_bf16(x_ref[...], g_ref[...])
    cos = cos_ref[...]
    sin = sin_ref[...]
    half = RET_DK // 2

    def chunks(base, width):
        for lo in range(0, width, PROJ_TN):
            yield lo, jnp.dot(xn, w_ref[:, base + lo:base + lo + PROJ_TN], preferred_element_type=F32)

    def rope_store(dst, base, mult):
        for lo, acc in chunks(base, RET_QK):
            for hh in range(PROJ_TN // RET_DK):
                c0 = hh * RET_DK
                x1 = acc[:, c0:c0 + half]
                x2 = acc[:, c0 + half:c0 + RET_DK]
                dst[:, lo + c0:lo + c0 + half] = ((x1 * cos - x2 * sin) * mult).astype(BF16)
                dst[:, lo + c0 + half:lo + c0 + RET_DK] = ((x2 * cos + x1 * sin) * mult).astype(BF16)

    def plain_store(dst, base):
        for lo, acc in chunks(base, RET_VW):
            dst[:, lo:lo + PROJ_TN] = acc.astype(BF16)

    rope_store(q_ref, 0, 1.0)
    rope_store(k_ref, RET_QK, RET_DK ** -0.5)
    plain_store(v_ref, 2 * RET_QK)
    plain_store(gate_ref, 2 * RET_QK + RET_VW)


def ret_in(x, g, w, cos, sin, t):
    m, d = x.shape
    tm = _pick(m, (512, 256))
    once = pl.Buffered(1)
    tab = _pos_table_spec(t, tm)
    out = lambda n: pl.BlockSpec((tm, n), lambda i: (i, 0))
    return pl.pallas_call(
        _ret_in_kernel,
        grid=(m // tm,),
        in_specs=[
            pl.BlockSpec((tm, d), lambda i: (i, 0)),
            pl.BlockSpec((1, d), lambda i: (0, 0), pipeline_mode=once),
            pl.BlockSpec((d, 2 * RET_QK + 2 * RET_VW), lambda i: (0, 0), pipeline_mode=once),
            tab, tab,
        ],
        out_specs=[out(RET_QK), out(RET_QK), out(RET_VW), out(RET_VW)],
        out_shape=[
            jax.ShapeDtypeStruct((m, RET_QK), BF16),
            jax.ShapeDtypeStruct((m, RET_QK), BF16),
            jax.ShapeDtypeStruct((m, RET_VW), BF16),
            jax.ShapeDtypeStruct((m, RET_VW), BF16),
        ],
        compiler_params=_cparams("parallel", vmem=VMEM_LIMIT_BIG),
        name="ret_in",
    )(x, g.reshape(1, d), w, _pos_table(cos, t, tm), _pos_table(sin, t, tm))


def _retention_kernel(lg_ref, q_ref, k_ref, v_ref, gate_ref, gn_ref, s0_ref, a_ref, s_ref, *, chunk, nc):
    lg = lg_ref[pl.program_id(1)]
    row = lax.broadcasted_iota(jnp.int32, (chunk, chunk), 0)
    col = lax.broadcasted_iota(jnp.int32, (chunk, chunk), 1)
    diff = row - col
    dmask = jnp.where(diff >= 0, jnp.exp(lg * jnp.maximum(diff, 0).astype(F32)), 0.0)
    idx = lax.broadcasted_iota(jnp.int32, (chunk, 1), 0).astype(F32)
    q_dec = jnp.exp(lg * (idx + 1.0))
    k_dec = jnp.exp(lg * (chunk - 1.0 - idx))
    c_dec = jnp.exp(lg * jnp.full((1, RET_DV), float(chunk), F32))
    gn = gn_ref[...]

    state = s0_ref[0, 0]
    for c in range(nc):
        rows = slice(c * chunk, (c + 1) * chunk)
        q = q_ref[0, rows, :]
        k = k_ref[0, rows, :]
        v = v_ref[0, rows, :]
        att = lax.dot_general(q, k, NT_DIMS, preferred_element_type=F32) * dmask
        qd = (q.astype(F32) * q_dec).astype(BF16)
        o = jnp.dot(att.astype(BF16), v, preferred_element_type=F32)
        o = o + jnp.dot(qd, state.astype(BF16), preferred_element_type=F32)
        kd = (k.astype(F32) * k_dec).astype(BF16)
        upd = lax.dot_general(kd, v, (((0,), (0,)), ((), ())), preferred_element_type=F32)
        state = state * c_dec + upd

        mu = jnp.mean(o, axis=-1, keepdims=True)
        oc = o - mu
        var = jnp.mean(oc * oc, axis=-1, keepdims=True)
        y = oc * lax.rsqrt(var + EPS) * gn
        g = gate_ref[0, rows, :].astype(F32)
        a_ref[0, rows, :] = (g * jax.nn.sigmoid(g) * y).astype(BF16)
    s_ref[0, 0] = state


def retention(q, k, v, gate, gn, s0, lg, chunk):
    b, t, _ = q.shape
    nc = t // chunk
    kern = functools.partial(_retention_kernel, chunk=chunk, nc=nc)
    tok = lambda bi, hi, lg: (bi, 0, hi)
    st = lambda bi, hi, lg: (bi, hi, 0, 0)
    return pl.pallas_call(
        kern,
        grid_spec=pltpu.PrefetchScalarGridSpec(
            num_scalar_prefetch=1,
            grid=(b, RET_HEADS),
            in_specs=[
                pl.BlockSpec((1, t, RET_DK), tok),
                pl.BlockSpec((1, t, RET_DK), tok),
                pl.BlockSpec((1, t, RET_DV), tok),
                pl.BlockSpec((1, t, RET_DV), tok),
                pl.BlockSpec((1, RET_DV), lambda bi, hi, lg: (0, hi)),
                pl.BlockSpec((1, 1, RET_DK, RET_DV), st),
            ],
            out_specs=[
                pl.BlockSpec((1, t, RET_DV), tok),
                pl.BlockSpec((1, 1, RET_DK, RET_DV), st),
            ],
        ),
        out_shape=[
            jax.ShapeDtypeStruct((b, t, RET_VW), BF16),
            jax.ShapeDtypeStruct((b, RET_HEADS, RET_DK, RET_DV), F32),
        ],
        compiler_params=_cparams("parallel", "parallel"),
        name="retention",
    )(lg, q, k, v, gate, gn, s0)


def _rope_lanes(x, c_tab, s1_tab, s2_tab):
    return x * c_tab + pltpu.roll(x, 3 * LANES // 4, 1) * s1_tab + pltpu.roll(x, LANES // 4, 1) * s2_tab


def _mla_q_kernel(p_ref, qn_ref, kvn_ref, gqn_ref, gqr_ref, gkr_ref, w_ref, c_ref, s1_ref, s2_ref,
                  q_ref, lat_ref, kr_ref):
    c_tab, s1_tab, s2_tab = c_ref[...], s1_ref[...], s2_ref[...]

    ckv = p_ref[:, MLA_Q_LORA:MLA_Q_LORA + MLA_KV_LORA]
    lat_ref[...] = ckv * lax.rsqrt(jnp.mean(ckv * ckv, axis=-1, keepdims=True) + EPS) * kvn_ref[...]

    krp = p_ref[:, MLA_Q_LORA + MLA_KV_LORA:MLA_IN_PAD]
    krn = krp * lax.rsqrt(jnp.sum(krp * krp, axis=-1, keepdims=True) * (1.0 / MLA_ROPE) + EPS) * gkr_ref[...]
    kr_ref[...] = _rope_lanes(krn, c_tab, s1_tab, s2_tab)

    cqn = _rms_bf16(p_ref[:, 0:MLA_Q_LORA], qn_ref[...])
    hg = 4
    for g0 in range(0, MLA_HEADS, hg):
        acc = jnp.dot(cqn, w_ref[:, g0 * MLA_DK_PAD:(g0 + hg) * MLA_DK_PAD], preferred_element_type=F32)
        for hh in range(hg):
            a = acc[:, hh * MLA_DK_PAD:hh * MLA_DK_PAD + LANES]
            r = acc[:, hh * MLA_DK_PAD + LANES:(hh + 1) * MLA_DK_PAD]
            an = a * lax.rsqrt(jnp.mean(a * a, axis=-1, keepdims=True) + EPS) * gqn_ref[...]
            rn = r * lax.rsqrt(jnp.sum(r * r, axis=-1, keepdims=True) * (1.0 / MLA_ROPE) + EPS) * gqr_ref[...]
            col = (g0 + hh) * MLA_DK_PAD
            q_ref[:, col:col + LANES] = an.astype(BF16)
            q_ref[:, col + LANES:col + MLA_DK_PAD] = _rope_lanes(rn, c_tab, s1_tab, s2_tab).astype(BF16)


def mla_q(proj, q_norm, kv_norm, gq_nope, gq_rope_pad, gk_rope_pad, w_qb_pad, tabs, t):
    m = proj.shape[0]
    tm = _pick(m, (512, 256))
    row = lambda n: pl.BlockSpec((1, n), lambda i: (0, 0))
    tab = _pos_table_spec(t, tm)
    return pl.pallas_call(
        _mla_q_kernel,
        grid=(m // tm,),
        in_specs=[
            pl.BlockSpec((tm, MLA_IN_PAD), lambda i: (i, 0)),
            row(MLA_Q_LORA), row(MLA_KV_LORA), row(LANES), row(LANES), row(LANES),
            pl.BlockSpec((MLA_Q_LORA, MLA_HEADS * MLA_DK_PAD), lambda i: (0, 0)),
            tab, tab, tab,
        ],
        out_specs=[
            pl.BlockSpec((tm, MLA_HEADS * MLA_DK_PAD), lambda i: (i, 0)),
            pl.BlockSpec((tm, MLA_KV_LORA), lambda i: (i, 0)),
            pl.BlockSpec((tm, LANES), lambda i: (i, 0)),
        ],
        out_shape=[
            jax.ShapeDtypeStruct((m, MLA_HEADS * MLA_DK_PAD), BF16),
            jax.ShapeDtypeStruct((m, MLA_KV_LORA), F32),
            jax.ShapeDtypeStruct((m, LANES), F32),
        ],
        compiler_params=_cparams("parallel"),
        name="mla_q",
    )(proj, q_norm.reshape(1, -1), kv_norm.reshape(1, -1), gq_nope.reshape(1, -1), gq_rope_pad, gk_rope_pad,
      w_qb_pad, *[_pos_table(x, t, tm) for x in tabs])


def _mla_kv_kernel(lat_ref, kr_ref, gk_ref, wk_ref, wv_ref, k_ref, v_ref):
    latb = lat_ref[...].astype(BF16)
    krb = kr_ref[...].astype(BF16)
    v_ref[...] = jnp.dot(latb, wv_ref[...], preferred_element_type=F32).astype(BF16)
    hg = 4
    for g0 in range(0, MLA_HEADS, hg):
        acc = jnp.dot(latb, wk_ref[:, g0 * MLA_NOPE:(g0 + hg) * MLA_NOPE], preferred_element_type=F32)
        for hh in range(hg):
            a = acc[:, hh * MLA_NOPE:(hh + 1) * MLA_NOPE]
            an = a * lax.rsqrt(jnp.mean(a * a, axis=-1, keepdims=True) + EPS) * gk_ref[...]
            col = (g0 + hh) * MLA_DK_PAD
            k_ref[:, col:col + LANES] = an.astype(BF16)
            k_ref[:, col + LANES:col + MLA_DK_PAD] = krb


def mla_kv(lat_all, kr_all, gk_nope, wk, wv):
    m = lat_all.shape[0]
    tm = _pick(m, (512, 256))
    return pl.pallas_call(
        _mla_kv_kernel,
        grid=(m // tm,),
        in_specs=[
            pl.BlockSpec((tm, MLA_KV_LORA), lambda i: (i, 0)),
            pl.BlockSpec((tm, LANES), lambda i: (i, 0)),
            pl.BlockSpec((1, MLA_NOPE), lambda i: (0, 0)),
            pl.BlockSpec((MLA_KV_LORA, MLA_HEADS * MLA_NOPE), lambda i: (0, 0)),
            pl.BlockSpec((MLA_KV_LORA, MLA_HEADS * MLA_V), lambda i: (0, 0)),
        ],
        out_specs=[
            pl.BlockSpec((tm, MLA_HEADS * MLA_DK_PAD), lambda i: (i, 0)),
            pl.BlockSpec((tm, MLA_HEADS * MLA_V), lambda i: (i, 0)),
        ],
        out_shape=[
            jax.ShapeDtypeStruct((m, MLA_HEADS * MLA_DK_PAD), BF16),
            jax.ShapeDtypeStruct((m, MLA_HEADS * MLA_V), BF16),
        ],
        compiler_params=_cparams("parallel"),
        name="mla_kv",
    )(lat_all, kr_all, gk_nope.reshape(1, -1), wk, wv)


def _head_norm(x, gain):
    lane = lax.broadcasted_iota(jnp.int32, (x.shape[0], LANES), 1)
    low = lane < FOX_DH
    outs = []
    for c in range(x.shape[1] // LANES):
        xb = x[:, c * LANES:(c + 1) * LANES]
        sq = xb * xb
        s_lo = jnp.sum(jnp.where(low, sq, 0.0), axis=-1, keepdims=True)
        s_hi = jnp.sum(jnp.where(low, 0.0, sq), axis=-1, keepdims=True)
        ms = jnp.where(low, s_lo, s_hi) * (1.0 / FOX_DH)
        outs.append(xb * lax.rsqrt(ms + EPS) * gain)
    return outs


def _fox_in_kernel(x_ref, g_ref, w_ref, gq_ref, gk_ref, bf_ref,
                   q_ref, k_ref, kb_ref, v_ref, vb_ref, gate_ref, lf_ref):
    xn = _rms_bf16(x_ref[...], g_ref[...])

    def chunks(seg):
        for lo in range(0, FOX_W, PROJ_TN):
            col = seg * FOX_W + lo
            yield lo, jnp.dot(xn, w_ref[:, col:col + PROJ_TN], preferred_element_type=F32)

    for lo, acc in chunks(0):
        for c, blk in enumerate(_head_norm(acc, gq_ref[...])):
            q_ref[:, lo + c * LANES:lo + (c + 1) * LANES] = (blk * (FOX_DH ** -0.5 * LOG2E)).astype(BF16)
    for lo, acc in chunks(1):
        for c, blk in enumerate(_head_norm(acc, gk_ref[...])):
            k_ref[:, lo + c * LANES:lo + (c + 1) * LANES] = blk
            kb_ref[:, lo + c * LANES:lo + (c + 1) * LANES] = blk.astype(BF16)
    for lo, acc in chunks(2):
        v_ref[:, lo:lo + PROJ_TN] = acc
        vb_ref[:, lo:lo + PROJ_TN] = acc.astype(BF16)
    for lo, acc in chunks(3):
        gate_ref[:, lo:lo + PROJ_TN] = acc.astype(BF16)
    fl = jnp.dot(xn, w_ref[:, 4 * FOX_W:4 * FOX_W + LANES], preferred_element_type=F32)
    lf_ref[...] = jax.nn.log_sigmoid(fl + bf_ref[...])


def fox_in(x, g, w, gq2, gk2, bf_pad):
    m, d = x.shape
    tm = _pick(m, (512, 256))
    once = pl.Buffered(1)
    seg = lambda: pl.BlockSpec((tm, FOX_W), lambda i: (i, 0))
    row = pl.BlockSpec((1, LANES), lambda i: (0, 0), pipeline_mode=once)
    return pl.pallas_call(
        _fox_in_kernel,
        grid=(m // tm,),
        in_specs=[
            pl.BlockSpec((tm, d), lambda i: (i, 0)),
            pl.BlockSpec((1, d), lambda i: (0, 0), pipeline_mode=once),
            pl.BlockSpec((d, 4 * FOX_W + LANES), lambda i: (0, 0), pipeline_mode=once),
            row, row, row,
        ],
        out_specs=[seg(), seg(), seg(), seg(), seg(), seg(),
                   pl.BlockSpec((tm, LANES), lambda i: (i, 0))],
        out_shape=[
            jax.ShapeDtypeStruct((m, FOX_W), BF16),
            jax.ShapeDtypeStruct((m, FOX_W), F32),
            jax.ShapeDtypeStruct((m, FOX_W), BF16),
            jax.ShapeDtypeStruct((m, FOX_W), F32),
            jax.ShapeDtypeStruct((m, FOX_W), BF16),
            jax.ShapeDtypeStruct((m, FOX_W), BF16),
            jax.ShapeDtypeStruct((m, LANES), F32),
        ],
        compiler_params=_cparams("parallel", vmem=VMEM_LIMIT_BIG),
        name="fox_in",
    )(x, g.reshape(1, d), w, gq2, gk2, bf_pad)


def _fox_cumsum_kernel(lf_ref, ft_ref, fh_ref):
    tkp = lf_ref.shape[1]
    x = lf_ref[0].T[0:FOX_HEADS, :]
    lane = lax.broadcasted_iota(jnp.int32, x.shape, 1)
    shift = 1
    while shift < tkp:
        x = x + jnp.where(lane >= shift, pltpu.roll(x, shift, 1), 0.0)
        shift *= 2
    x = x * LOG2E
    for h in range(FOX_HEADS):
        fh_ref[0, h] = x[h:h + 1, :]
    ft_ref[0] = jnp.concatenate([x, jnp.zeros((LANES - FOX_HEADS, tkp), F32)], axis=0).T


def fox_cumsum(lf_all):
    b, tkp, _ = lf_all.shape
    return pl.pallas_call(
        _fox_cumsum_kernel,
        grid=(b,),
        in_specs=[pl.BlockSpec((1, tkp, LANES), lambda i: (i, 0, 0))],
        out_specs=[pl.BlockSpec((1, tkp, LANES), lambda i: (i, 0, 0)),
                   pl.BlockSpec((1, FOX_HEADS, 1, tkp), lambda i: (i, 0, 0, 0))],
        out_shape=[jax.ShapeDtypeStruct((b, tkp, LANES), F32),
                   jax.ShapeDtypeStruct((b, FOX_HEADS, 1, tkp), F32)],
        compiler_params=_cparams("parallel"),
        name="fox_cumsum",
    )(lf_all)


def _rope_tables(pos, half):
    inv = ROPE_THETA ** (-jnp.arange(half, dtype=F32) / half)
    ang = pos.astype(F32)[:, None] * inv[None, :]
    return jnp.cos(ang), jnp.sin(ang)


def _pad_rows(a, rows):
    pad = rows - a.shape[1]
    if pad == 0:
        return a
    return jnp.pad(a, ((0, 0), (0, pad)) + ((0, 0),) * (a.ndim - 2))


def _retention_mixer(h2, b, t, pos, s0, gmix, w_in, gn, w_out):
    cos, sin = _rope_tables(pos, RET_DK // 2)
    q, k, v, gate = ret_in(h2, gmix, w_in, cos, sin, t)
    lg = jnp.log1p(-jnp.exp2(-5.0 - jnp.arange(RET_HEADS, dtype=F32)))
    chunk = _pick(t, (256, 128, 64))
    a, s = retention(q.reshape(b, t, RET_QK), k.reshape(b, t, RET_QK), v.reshape(b, t, RET_VW),
                     gate.reshape(b, t, RET_VW), gn.reshape(1, RET_VW), s0, lg, chunk)
    return matmul_res(a.reshape(b * t, RET_VW), w_out, h2, name="ret_out"), s


def _mla_mixer(h2, b, t, pos, lat_past, kr_past, gmix, w):
    proj = norm_matmul(h2, gmix, w['w_in'], name="mla_in")
    cos, sin = _rope_tables(pos, MLA_ROPE // 2)
    z = jnp.zeros_like(cos)
    tabs = (jnp.concatenate([cos, cos, z, z], axis=1), jnp.concatenate([-sin, z, z, z], axis=1),
            jnp.concatenate([z, sin, z, z], axis=1))
    q_pad, lat, kr_pad = mla_q(proj, w['q_norm'], w['kv_norm'], w['gq_nope'], w['gq_rope_pad'], w['gk_rope_pad'],
                               w['w_qb_pad'], tabs, t)
    tk = lat_past.shape[1] + t
    tkp = -(-tk // LANES) * LANES
    if lat_past.shape[1] == 0:
        lat_all, kr_all = lat, kr_pad
    else:
        lat_all = _pad_rows(jnp.concatenate([lat_past, lat.reshape(b, t, MLA_KV_LORA)], axis=1), tkp)
        kr_past_pad = jnp.pad(kr_past, ((0, 0), (0, 0), (0, LANES - MLA_ROPE)))
        kr_all = _pad_rows(jnp.concatenate([kr_past_pad, kr_pad.reshape(b, t, LANES)], axis=1), tkp)
        lat_all = lat_all.reshape(b * tkp, MLA_KV_LORA)
        kr_all = kr_all.reshape(b * tkp, LANES)
    k_pad, v = mla_kv(lat_all, kr_all, w['gk_nope'], w['w_kb'], w['w_vb'])
    o = attention(q_pad.reshape(b, t, -1), k_pad.reshape(b, tkp, -1), v.reshape(b, tkp, -1), mode="chunk",
                  n_blocks=MLA_HEADS // ATTN_NSUB, nsub=ATTN_NSUB, hpb=1, tk_real=tk, name="mla_attn")
    h2 = matmul_res(o.reshape(b * t, MLA_HEADS * MLA_V), w['w_out'], h2, name="mla_out")
    return h2, lat.reshape(b, t, MLA_KV_LORA), kr_pad[:, :MLA_ROPE].reshape(b, t, MLA_ROPE)


def _fox_mixer(h2, b, t, k_past, v_past, lf_past, gmix, w):
    q, k, kb, v, vb, gate, lf_pad = fox_in(h2, gmix, w['w_in'], w['gq2'], w['gk2'], w['bf_pad'])
    lf = lf_pad[:, :FOX_HEADS].reshape(b, t, FOX_HEADS)
    tp = k_past.shape[1]
    tk = tp + t
    tkp = -(-tk // LANES) * LANES
    kb = kb.reshape(b, t, FOX_W)
    vb = vb.reshape(b, t, FOX_W)
    if tp > 0:
        kb = _pad_rows(jnp.concatenate([k_past.reshape(b, tp, FOX_W).astype(BF16), kb], axis=1), tkp)
        vb = _pad_rows(jnp.concatenate([v_past.reshape(b, tp, FOX_W).astype(BF16), vb], axis=1), tkp)
    lf_all = lf_pad.reshape(b, t, LANES)
    if tp > 0:
        lf_all = _pad_rows(jnp.concatenate([_pad_last(lf_past.astype(F32), LANES), lf_all], axis=1), tkp)
    f_time, fk = fox_cumsum(lf_all)
    fq = f_time[:, tk - t:tk]
    a = attention(q.reshape(b, t, FOX_W), kb, vb, mode="token", n_blocks=FOX_HEADS // (2 * ATTN_NSUB),
                  nsub=ATTN_NSUB, hpb=2, tk_real=tk, name="fox_attn", fq=fq, fk=fk,
                  gate=gate.reshape(b, t, FOX_W))
    h2 = matmul_res(a.reshape(b * t, FOX_W), w['w_out'], h2, name="fox_out")
    return (h2, k.reshape(b, t, FOX_HEADS, FOX_DH), v.reshape(b, t, FOX_HEADS, FOX_DH), lf)


def _trunk(x, p, past_len, ret_states, lat_past, kr_past, fk_past, fv_past, flf_past, w):
    b, t, d = x.shape
    pos = past_len + jnp.arange(t)
    h2 = x.reshape(b * t, d)
    new_ret, new_lat, new_kr, new_fk, new_fv, new_flf = [], [], [], [], [], []
    for i in range(DEPTH):
        kind, j = i % N_MIXERS, i // N_MIXERS
        if kind == 0:
            h2, s = _retention_mixer(h2, b, t, pos, ret_states[j], w['norm_mix'][i], w['ret_w_in'][j],
                                     w['ret_gn'][j], w['ret_w_out'][j])
            new_ret.append(s)
        elif kind == 1:
            h2, lat, kr = _mla_mixer(h2, b, t, pos, lat_past[j], kr_past[j], w['norm_mix'][i], w['mla'][j])
            new_lat.append(lat)
            new_kr.append(kr)
        else:
            h2, fk, fv, flf = _fox_mixer(h2, b, t, fk_past[j], fv_past[j], flf_past[j], w['norm_mix'][i],
                                         w['fox'][j])
            new_fk.append(fk)
            new_fv.append(fv)
            new_flf.append(flf)
        h2 = ffn_pe(h2, p.reshape(DEPTH, b * t, PE_DIM), i, w['norm_ffn'][i], w['ffn_w_gate'][i],
                    w['ffn_w_up'][i], w['ffn_w_down'][i], w['norm_pe'][i], w['pe_w_gate'][i], w['pe_w_proj'][i],
                    final_gain=w['norm_final'] if i == DEPTH - 1 else None)
    return (h2.reshape(b, t, d), jnp.stack(new_ret), jnp.stack(new_lat), jnp.stack(new_kr),
            jnp.stack(new_fk), jnp.stack(new_fv), jnp.stack(new_flf))


def _pad_last(a, n):
    return jnp.pad(a, [(0, 0)] * (a.ndim - 1) + [(0, n - a.shape[-1])])


def kernel(x_prompt, x_sample, state_ret, cache_mla_latent, cache_mla_krope, cache_fox_k, cache_fox_v,
           cache_fox_logf, p_prompt, p_sample, norm_mix, norm_ffn, norm_pe, norm_final, ret_w_in, ret_gn,
           ret_w_out, mla_w_in, mla_q_norm, mla_kv_norm, mla_w_qb, mla_w_kvb, mla_gq_nope, mla_gq_rope,
           mla_gk_nope, mla_gk_rope, mla_w_out, fox_w_in, fox_b_f, fox_gq, fox_gk, fox_w_out, ffn_w_gate,
           ffn_w_up, ffn_w_down, pe_w_proj, pe_w_gate):
    bf = lambda a: a.astype(BF16)
    n_ret, n_mla, n_fox = state_ret.shape[0], cache_mla_latent.shape[0], cache_fox_k.shape[0]
    mla = []
    qscale = (MLA_NOPE + MLA_ROPE) ** -0.5 * LOG2E
    for j in range(n_mla):
        w_qb = mla_w_qb[j].reshape(MLA_Q_LORA, MLA_HEADS, MLA_NOPE + MLA_ROPE)
        w_kvb = mla_w_kvb[j].reshape(MLA_KV_LORA, MLA_HEADS, MLA_NOPE + MLA_V)
        mla.append(dict(
            w_in=bf(_pad_last(mla_w_in[j], MLA_IN_PAD)),
            q_norm=mla_q_norm[j], kv_norm=mla_kv_norm[j], gk_nope=mla_gk_nope[j],
            gq_nope=mla_gq_nope[j] * qscale,
            gq_rope_pad=_pad_last(mla_gq_rope[j] * qscale, LANES).reshape(1, LANES),
            gk_rope_pad=_pad_last(mla_gk_rope[j], LANES).reshape(1, LANES),
            w_qb_pad=bf(_pad_last(w_qb, MLA_DK_PAD).reshape(MLA_Q_LORA, MLA_HEADS * MLA_DK_PAD)),
            w_kb=bf(w_kvb[:, :, :MLA_NOPE].reshape(MLA_KV_LORA, MLA_HEADS * MLA_NOPE)),
            w_vb=bf(w_kvb[:, :, MLA_NOPE:].reshape(MLA_KV_LORA, MLA_HEADS * MLA_V)),
            w_out=bf(mla_w_out[j])))
    fox = []
    for j in range(n_fox):
        fox.append(dict(
            w_in=bf(_pad_last(fox_w_in[j], 4 * FOX_W + LANES)),
            gq2=jnp.tile(fox_gq[j], 2).reshape(1, LANES), gk2=jnp.tile(fox_gk[j], 2).reshape(1, LANES),
            bf_pad=_pad_last(fox_b_f[j], LANES).reshape(1, LANES),
            w_out=bf(fox_w_out[j])))
    w = dict(norm_mix=norm_mix, norm_ffn=norm_ffn, norm_pe=norm_pe, norm_final=norm_final,
             ret_w_in=bf(ret_w_in), ret_gn=ret_gn, ret_w_out=bf(ret_w_out), mla=mla, fox=fox,
             ffn_w_gate=bf(ffn_w_gate), ffn_w_up=bf(ffn_w_up), ffn_w_down=bf(ffn_w_down),
             pe_w_proj=bf(pe_w_proj), pe_w_gate=bf(pe_w_gate))
    bp = x_prompt.shape[0]
    dt = x_prompt.dtype
    y_p, ret_p, lat_p, kr_p, fk_p, fv_p, flf_p = _trunk(
        x_prompt, p_prompt, 0,
        jnp.zeros((n_ret, bp, RET_HEADS, RET_DK, RET_DV), dt),
        jnp.zeros((n_mla, bp, 0, MLA_KV_LORA), dt), jnp.zeros((n_mla, bp, 0, MLA_ROPE), dt),
        jnp.zeros((n_fox, bp, 0, FOX_HEADS, FOX_DH), dt), jnp.zeros((n_fox, bp, 0, FOX_HEADS, FOX_DH), dt),
        jnp.zeros((n_fox, bp, 0, FOX_HEADS), dt), w)
    y_s, ret_s, lat_s, kr_s, fk_s, fv_s, flf_s = _trunk(
        x_sample, p_sample, cache_mla_latent.shape[2], state_ret, cache_mla_latent, cache_mla_krope,
        cache_fox_k, cache_fox_v, cache_fox_logf, w)
    return (y_p, y_s, ret_p, ret_s, lat_p, kr_p, lat_s, kr_s, fk_p, fv_p, flf_p, fk_s, fv_s, flf_s)
```

```python
import functools
import math

import jax
import jax.numpy as jnp
from jax import lax
from jax.experimental import pallas as pl
from jax.experimental.pallas import tpu as pltpu

D_MODEL = 1024
DEPTH = 4
CHUNK = 64
PE_DIM = 256
N_MIXERS = 3
EPS = 1e-6
ROPE_THETA = 10000.0
RET_HEADS = 4
RET_DK = 256
RET_DV = 512
RET_QK = RET_HEADS * RET_DK
RET_VW = RET_HEADS * RET_DV
MLA_HEADS = 16
MLA_NOPE = 128
MLA_ROPE = 64
MLA_V = 128
MLA_Q_LORA = 512
MLA_KV_LORA = 256
FOX_HEADS = 16
FOX_DH = 64
FOX_W = FOX_HEADS * FOX_DH
D_FF = ((8 * D_MODEL + 3 * 256 - 1) // (3 * 256)) * 256

LANES = 128
MLA_DK_PAD = 2 * LANES
MLA_IN_PAD = 7 * LANES
PROJ_TN = 512
ATTN_NSUB = 2
MASK_VALUE = -1e30
LOG2E = math.log2(math.e)
VMEM_LIMIT = 48 * 1024 * 1024
VMEM_LIMIT_BIG = 56 * 1024 * 1024

F32 = jnp.float32
BF16 = jnp.bfloat16
NT_DIMS = (((1,), (1,)), ((), ()))


def _cparams(*sem, vmem=VMEM_LIMIT):
    return pltpu.CompilerParams(dimension_semantics=sem, vmem_limit_bytes=vmem)


def _pick(n, prefs):
    for p in prefs:
        if n % p == 0:
            return p
    return n


def _rms_bf16(x, g):
    ms = jnp.mean(x * x, axis=-1, keepdims=True)
    return (x * lax.rsqrt(ms + EPS) * g).astype(BF16)


def _pos_table_spec(t, tm):
    if t % tm == 0:
        nt = t // tm
        return pl.BlockSpec((tm, LANES), lambda i, *_: (i % nt, 0))
    return pl.BlockSpec((tm, LANES), lambda i, *_: (0, 0))


def _pos_table(tab, t, tm):
    return tab if t % tm == 0 else jnp.tile(tab, (tm // t, 1))


def _norm_mm_kernel(x_ref, g_ref, w_ref, o_ref, xn_ref):
    @pl.when(pl.program_id(1) == 0)
    def _():
        xn_ref[...] = _rms_bf16(x_ref[...], g_ref[...])

    o_ref[...] = jnp.dot(xn_ref[...], w_ref[...], preferred_element_type=F32).astype(o_ref.dtype)


def norm_matmul(x, g, w, out_dtype=F32, name="norm_mm"):
    m, k = x.shape
    n = w.shape[1]
    tm = _pick(m, (1024, 512, 256))
    tn = _pick(n, (1024, 768, 512, 896, 128))
    return pl.pallas_call(
        _norm_mm_kernel,
        grid=(m // tm, n // tn),
        in_specs=[
            pl.BlockSpec((tm, k), lambda i, j: (i, 0)),
            pl.BlockSpec((1, k), lambda i, j: (0, 0)),
            pl.BlockSpec((k, tn), lambda i, j: (0, j)),
        ],
        out_specs=pl.BlockSpec((tm, tn), lambda i, j: (i, j)),
        out_shape=jax.ShapeDtypeStruct((m, n), out_dtype),
        scratch_shapes=[pltpu.VMEM((tm, k), BF16)],
        compiler_params=_cparams("parallel", "arbitrary"),
        name=name,
    )(x, g.reshape(1, k), w)


def _mm_res_kernel(a_ref, w_ref, r_ref, o_ref):
    o_ref[...] = r_ref[...] + jnp.dot(a_ref[...], w_ref[...], preferred_element_type=F32)


def matmul_res(a, w, res, name="mm"):
    m, k = a.shape
    n = w.shape[1]
    tm = _pick(m, (1024, 512, 256))
    tn = _pick(n, (1024, 512, 128))
    return pl.pallas_call(
        _mm_res_kernel,
        grid=(m // tm, n // tn),
        in_specs=[
            pl.BlockSpec((tm, k), lambda i, j: (i, 0)),
            pl.BlockSpec((k, tn), lambda i, j: (0, j)),
            pl.BlockSpec((tm, tn), lambda i, j: (i, j)),
        ],
        out_specs=pl.BlockSpec((tm, tn), lambda i, j: (i, j)),
        out_shape=jax.ShapeDtypeStruct((m, n), F32),
        compiler_params=_cparams("parallel", "arbitrary"),
        name=name,
    )(a, w, res)


def _ffn_pe_kernel(x_ref, p_ref, g_ref, wg_ref, wu_ref, wd_ref, gp_ref, wpg_ref, wpp_ref, *rest, final):
    o_ref = rest[-1]
    x = x_ref[...]
    xn = _rms_bf16(x, g_ref[...])
    o_ref[...] = x
    dff = wg_ref.shape[1]
    for lo in range(0, dff, PROJ_TN):
        wd_rows = min(PROJ_TN, dff - lo)
        gate = jnp.dot(xn, wg_ref[:, lo:lo + wd_rows], preferred_element_type=F32)
        up = jnp.dot(xn, wu_ref[:, lo:lo + wd_rows], preferred_element_type=F32)
        act = (gate * jax.nn.sigmoid(gate) * up).astype(BF16)
        o_ref[...] += jnp.dot(act, wd_ref[lo:lo + wd_rows, :], preferred_element_type=F32)

    h = o_ref[...]
    hn = _rms_bf16(h, gp_ref[...])
    pb = p_ref[...].astype(BF16)
    for lo in range(0, h.shape[1], PROJ_TN):
        gate = jax.nn.sigmoid(jnp.dot(hn, wpg_ref[:, lo:lo + PROJ_TN], preferred_element_type=F32))
        proj = jnp.dot(pb, wpp_ref[:, lo:lo + PROJ_TN], preferred_element_type=F32)
        o_ref[:, lo:lo + PROJ_TN] = h[:, lo:lo + PROJ_TN] + gate * proj
    if final:
        out = o_ref[...]
        ms = jnp.mean(out * out, axis=-1, keepdims=True)
        o_ref[...] = out * lax.rsqrt(ms + EPS) * rest[0][...]


def ffn_pe(h, p, layer, g, wg, wu, wd, gp, wpg, wpp, final_gain=None):
    m, d = h.shape
    dff = wg.shape[1]
    pd = p.shape[2]
    tm = _pick(m, (512, 256))
    once = pl.Buffered(1)
    row = pl.BlockSpec((1, d), lambda i: (0, 0), pipeline_mode=once)
    in_specs = [
        pl.BlockSpec((tm, d), lambda i: (i, 0)),
        pl.BlockSpec((None, tm, pd), lambda i: (layer, i, 0)),
        row,
        pl.BlockSpec((d, dff), lambda i: (0, 0), pipeline_mode=once),
        pl.BlockSpec((d, dff), lambda i: (0, 0), pipeline_mode=once),
        pl.BlockSpec((dff, d), lambda i: (0, 0), pipeline_mode=once),
        row,
        pl.BlockSpec((d, d), lambda i: (0, 0), pipeline_mode=once),
        pl.BlockSpec((pd, d), lambda i: (0, 0), pipeline_mode=once),
    ]
    args = [h, p, g.reshape(1, d), wg, wu, wd, gp.reshape(1, d), wpg, wpp]
    if final_gain is not None:
        in_specs.append(row)
        args.append(final_gain.reshape(1, d))
    return pl.pallas_call(
        functools.partial(_ffn_pe_kernel, final=final_gain is not None),
        grid=(m // tm,),
        in_specs=in_specs,
        out_specs=pl.BlockSpec((tm, d), lambda i: (i, 0)),
        out_shape=jax.ShapeDtypeStruct((m, d), F32),
        compiler_params=_cparams("parallel", vmem=VMEM_LIMIT_BIG),
        name="ffn_pe",
    )(*args)


def _attn_kernel(*refs, tq, nq, tkp, mw, nsub, hpb, has_bias, has_gate):
    q_ref, k_ref, v_ref, mb_ref = refs[:4]
    pos = 4
    if has_bias:
        fq_ref, fk_ref = refs[4:6]
        pos = 6
    if has_gate:
        g_ref = refs[pos]
        pos += 1
    o_ref = refs[pos]
    bw_k = q_ref.shape[2] // nsub
    bw_v = v_ref.shape[2] // nsub
    mb = mb_ref[...]
    lane_k = lax.broadcasted_iota(jnp.int32, (tq, bw_k), 1)
    lane_v = lax.broadcasted_iota(jnp.int32, (tq, bw_v), 1)

    for qi in reversed(range(nq)):
        r0 = qi * tq
        kv_len = tkp - (nq - 1 - qi) * tq
        head_len = kv_len - mw
        for sb in range(nsub):
            kc = slice(sb * bw_k, (sb + 1) * bw_k)
            vc = slice(sb * bw_v, (sb + 1) * bw_v)
            q_blk = q_ref[0, r0:r0 + tq, kc]
            out = None
            for hh in range(hpb):
                q = q_blk if hpb == 1 else jnp.where(lane_k // (bw_k // hpb) == hh, q_blk, jnp.zeros_like(q_blk))
                hb = sb * hpb + hh

                def scores(lo, hi):
                    s = lax.dot_general(q, k_ref[0, lo:hi, kc], NT_DIMS, preferred_element_type=F32)
                    if has_bias:
                        s = s - fk_ref[0, hb, :, lo:hi]
                    return s

                s_tail = scores(head_len, kv_len) + mb
                m = jnp.max(s_tail, axis=1, keepdims=True)
                if head_len > 0:
                    s_head = scores(0, head_len)
                    m = jnp.maximum(m, jnp.max(s_head, axis=1, keepdims=True))
                if has_bias:
                    head = pl.program_id(1) * (nsub * hpb) + hb
                    f_rows = fq_ref[0, r0:r0 + tq, :]
                    lane_f = lax.broadcasted_iota(jnp.int32, f_rows.shape, 1)
                    fq = jnp.sum(jnp.where(lane_f == head, f_rows, 0.0), axis=1, keepdims=True)
                    shift = fq - (fq + m)
                else:
                    shift = -m
                p_tail = jnp.exp2(s_tail + shift)
                l = jnp.sum(p_tail, axis=1, keepdims=True)
                pv = jnp.dot(p_tail.astype(BF16), v_ref[0, head_len:kv_len, vc], preferred_element_type=F32)
                if head_len > 0:
                    p_head = jnp.exp2(s_head + shift)
                    l = l + jnp.sum(p_head, axis=1, keepdims=True)
                    pv = pv + jnp.dot(p_head.astype(BF16), v_ref[0, 0:head_len, vc], preferred_element_type=F32)
                o_h = pv * (1.0 / l)
                out = o_h if hh == 0 else jnp.where(lane_v // (bw_v // hpb) == hh, o_h, out)
            if has_gate:
                out = jax.nn.sigmoid(g_ref[0, r0:r0 + tq, vc].astype(F32)) * out
            o_ref[0, r0:r0 + tq, vc] = out.astype(o_ref.dtype)


def attention(q, k, v, *, mode, n_blocks, nsub, hpb, tk_real, name, fq=None, fk=None, gate=None):
    b, tq_all, wq = q.shape
    tkp = k.shape[1]
    bw_k = wq // n_blocks
    bw_v = v.shape[2] // n_blocks
    tq = _pick(tq_all, (512,))
    nq = tq_all // tq
    mw = tq if nq > 1 or tq_all == tkp else LANES
    q0 = tk_real - tq
    k0 = tkp - mw
    qpos = q0 + lax.broadcasted_iota(jnp.int32, (tq, mw), 0)
    kpos = k0 + lax.broadcasted_iota(jnp.int32, (tq, mw), 1)
    allowed = (kpos // CHUNK <= qpos // CHUNK) if mode == "chunk" else (kpos <= qpos)
    allowed = jnp.logical_and(allowed, kpos < tk_real)
    mb = jnp.where(allowed, 0.0, MASK_VALUE).astype(F32)

    in_specs = [
        pl.BlockSpec((1, tq_all, bw_k), lambda bi, hi: (bi, 0, hi)),
        pl.BlockSpec((1, tkp, bw_k), lambda bi, hi: (bi, 0, hi)),
        pl.BlockSpec((1, tkp, bw_v), lambda bi, hi: (bi, 0, hi)),
        pl.BlockSpec((tq, mw), lambda bi, hi: (0, 0)),
    ]
    args = [q, k, v, mb]
    if fq is not None:
        in_specs += [
            pl.BlockSpec((1, tq_all, LANES), lambda bi, hi: (bi, 0, 0)),
            pl.BlockSpec((1, nsub * hpb, 1, tkp), lambda bi, hi: (bi, hi, 0, 0)),
        ]
        args += [fq, fk]
    if gate is not None:
        in_specs.append(pl.BlockSpec((1, tq_all, bw_v), lambda bi, hi: (bi, 0, hi)))
        args.append(gate)
    kern = functools.partial(_attn_kernel, tq=tq, nq=nq, tkp=tkp, mw=mw, nsub=nsub, hpb=hpb,
                             has_bias=fq is not None, has_gate=gate is not None)
    return pl.pallas_call(
        kern,
        grid=(b, n_blocks),
        in_specs=in_specs,
        out_specs=pl.BlockSpec((1, tq_all, bw_v), lambda bi, hi: (bi, 0, hi)),
        out_shape=jax.ShapeDtypeStruct((b, tq_all, n_blocks * bw_v), BF16),
        compiler_params=_cparams("parallel", "parallel"),
        name=name,
    )(*args)


def _ret_in_kernel(x_ref, g_ref, w_ref, cos_ref, sin_ref, q_ref, k_ref, v_ref, gate_ref):
    xn = _rms_bf16(x_ref[...], g_ref[...])
    cos = cos_ref[...]
    sin = sin_ref[...]
    half = RET_DK // 2

    def chunks(base, width):
        for lo in range(0, width, PROJ_TN):
            yield lo, jnp.dot(xn, w_ref[:, base + lo:base + lo + PROJ_TN], preferred_element_type=F32)

    def rope_store(dst, base, mult):
        for lo, acc in chunks(base, RET_QK):
            for hh in range(PROJ_TN // RET_DK):
                c0 = hh * RET_DK
                x1 = acc[:, c0:c0 + half]
                x2 = acc[:, c0 + half:c0 + RET_DK]
                dst[:, lo + c0:lo + c0 + half] = ((x1 * cos - x2 * sin) * mult).astype(BF16)
                dst[:, lo + c0 + half:lo + c0 + RET_DK] = ((x2 * cos + x1 * sin) * mult).astype(BF16)

    def plain_store(dst, base):
        for lo, acc in chunks(base, RET_VW):
            dst[:, lo:lo + PROJ_TN] = acc.astype(BF16)

    rope_store(q_ref, 0, 1.0)
    rope_store(k_ref, RET_QK, RET_DK ** -0.5)
    plain_store(v_ref, 2 * RET_QK)
    plain_store(gate_ref, 2 * RET_QK + RET_VW)


def ret_in(x, g, w, cos, sin, t):
    m, d = x.shape
    tm = _pick(m, (512, 256))
    once = pl.Buffered(1)
    tab = _pos_table_spec(t, tm)
    out = lambda n: pl.BlockSpec((tm, n), lambda i: (i, 0))
    return pl.pallas_call(
        _ret_in_kernel,
        grid=(m // tm,),
        in_specs=[
            pl.BlockSpec((tm, d), lambda i: (i, 0)),
            pl.BlockSpec((1, d), lambda i: (0, 0), pipeline_mode=once),
            pl.BlockSpec((d, 2 * RET_QK + 2 * RET_VW), lambda i: (0, 0), pipeline_mode=once),
            tab, tab,
        ],
        out_specs=[out(RET_QK), out(RET_QK), out(RET_VW), out(RET_VW)],
        out_shape=[
            jax.ShapeDtypeStruct((m, RET_QK), BF16),
            jax.ShapeDtypeStruct((m, RET_QK), BF16),
            jax.ShapeDtypeStruct((m, RET_VW), BF16),
            jax.ShapeDtypeStruct((m, RET_VW), BF16),
        ],
        compiler_params=_cparams("parallel", vmem=VMEM_LIMIT_BIG),
        name="ret_in",
    )(x, g.reshape(1, d), w, _pos_table(cos, t, tm), _pos_table(sin, t, tm))


def _retention_kernel(lg_ref, q_ref, k_ref, v_ref, gate_ref, gn_ref, s0_ref, a_ref, s_ref, *, chunk, nc):
    lg = lg_ref[pl.program_id(1)]
    row = lax.broadcasted_iota(jnp.int32, (chunk, chunk), 0)
    col = lax.broadcasted_iota(jnp.int32, (chunk, chunk), 1)
    diff = row - col
    dmask = jnp.where(diff >= 0, jnp.exp(lg * jnp.maximum(diff, 0).astype(F32)), 0.0)
    idx = lax.broadcasted_iota(jnp.int32, (chunk, 1), 0).astype(F32)
    q_dec = jnp.exp(lg * (idx + 1.0))
    k_dec = jnp.exp(lg * (chunk - 1.0 - idx))
    c_dec = jnp.exp(lg * jnp.full((1, RET_DV), float(chunk), F32))
    gn = gn_ref[...]

    state = s0_ref[0, 0]
    for c in range(nc):
        rows = slice(c * chunk, (c + 1) * chunk)
        q = q_ref[0, rows, :]
        k = k_ref[0, rows, :]
        v = v_ref[0, rows, :]
        att = lax.dot_general(q, k, NT_DIMS, preferred_element_type=F32) * dmask
        qd = (q.astype(F32) * q_dec).astype(BF16)
        o = jnp.dot(att.astype(BF16), v, preferred_element_type=F32)
        o = o + jnp.dot(qd, state.astype(BF16), preferred_element_type=F32)
        kd = (k.astype(F32) * k_dec).astype(BF16)
        upd = lax.dot_general(kd, v, (((0,), (0,)), ((), ())), preferred_element_type=F32)
        state = state * c_dec + upd

        mu = jnp.mean(o, axis=-1, keepdims=True)
        oc = o - mu
        var = jnp.mean(oc * oc, axis=-1, keepdims=True)
        y = oc * lax.rsqrt(var + EPS) * gn
        g = gate_ref[0, rows, :].astype(F32)
        a_ref[0, rows, :] = (g * jax.nn.sigmoid(g) * y).astype(BF16)
    s_ref[0, 0] = state


def retention(q, k, v, gate, gn, s0, lg, chunk):
    b, t, _ = q.shape
    nc = t // chunk
    kern = functools.partial(_retention_kernel, chunk=chunk, nc=nc)
    tok = lambda bi, hi, lg: (bi, 0, hi)
    st = lambda bi, hi, lg: (bi, hi, 0, 0)
    return pl.pallas_call(
        kern,
        grid_spec=pltpu.PrefetchScalarGridSpec(
            num_scalar_prefetch=1,
            grid=(b, RET_HEADS),
            in_specs=[
                pl.BlockSpec((1, t, RET_DK), tok),
                pl.BlockSpec((1, t, RET_DK), tok),
                pl.BlockSpec((1, t, RET_DV), tok),
                pl.BlockSpec((1, t, RET_DV), tok),
                pl.BlockSpec((1, RET_DV), lambda bi, hi, lg: (0, hi)),
                pl.BlockSpec((1, 1, RET_DK, RET_DV), st),
            ],
            out_specs=[
                pl.BlockSpec((1, t, RET_DV), tok),
                pl.BlockSpec((1, 1, RET_DK, RET_DV), st),
            ],
        ),
        out_shape=[
            jax.ShapeDtypeStruct((b, t, RET_VW), BF16),
            jax.ShapeDtypeStruct((b, RET_HEADS, RET_DK, RET_DV), F32),
        ],
        compiler_params=_cparams("parallel", "parallel"),
        name="retention",
    )(lg, q, k, v, gate, gn, s0)


def _rope_lanes(x, c_tab, s1_tab, s2_tab):
    return x * c_tab + pltpu.roll(x, 3 * LANES // 4, 1) * s1_tab + pltpu.roll(x, LANES // 4, 1) * s2_tab


def _mla_q_kernel(x_ref, gm_ref, win_ref, qn_ref, kvn_ref, gqn_ref, gqr_ref, gkr_ref, w_ref, c_ref, s1_ref, s2_ref,
                  q_ref, lat_ref, kr_ref):
    c_tab, s1_tab, s2_tab = c_ref[...], s1_ref[...], s2_ref[...]
    proj = jnp.dot(_rms_bf16(x_ref[...], gm_ref[...]), win_ref[...], preferred_element_type=F32)

    ckv = proj[:, MLA_Q_LORA:MLA_Q_LORA + MLA_KV_LORA]
    lat_ref[...] = ckv * lax.rsqrt(jnp.mean(ckv * ckv, axis=-1, keepdims=True) + EPS) * kvn_ref[...]

    krp = proj[:, MLA_Q_LORA + MLA_KV_LORA:MLA_IN_PAD]
    krn = krp * lax.rsqrt(jnp.sum(krp * krp, axis=-1, keepdims=True) * (1.0 / MLA_ROPE) + EPS) * gkr_ref[...]
    kr_ref[...] = _rope_lanes(krn, c_tab, s1_tab, s2_tab)

    cqn = _rms_bf16(proj[:, 0:MLA_Q_LORA], qn_ref[...])
    hg = 4
    for g0 in range(0, MLA_HEADS, hg):
        acc = jnp.dot(cqn, w_ref[:, g0 * MLA_DK_PAD:(g0 + hg) * MLA_DK_PAD], preferred_element_type=F32)
        for hh in range(hg):
            a = acc[:, hh * MLA_DK_PAD:hh * MLA_DK_PAD + LANES]
            r = acc[:, hh * MLA_DK_PAD + LANES:(hh + 1) * MLA_DK_PAD]
            an = a * lax.rsqrt(jnp.mean(a * a, axis=-1, keepdims=True) + EPS) * gqn_ref[...]
            rn = r * lax.rsqrt(jnp.sum(r * r, axis=-1, keepdims=True) * (1.0 / MLA_ROPE) + EPS) * gqr_ref[...]
            col = (g0 + hh) * MLA_DK_PAD
            q_ref[:, col:col + LANES] = an.astype(BF16)
            q_ref[:, col + LANES:col + MLA_DK_PAD] = _rope_lanes(rn, c_tab, s1_tab, s2_tab).astype(BF16)


def mla_q(x, gmix, w_in, q_norm, kv_norm, gq_nope, gq_rope_pad, gk_rope_pad, w_qb_pad, tabs, t):
    m, d = x.shape
    tm = _pick(m, (512, 256))
    row = lambda n: pl.BlockSpec((1, n), lambda i: (0, 0))
    tab = _pos_table_spec(t, tm)
    return pl.pallas_call(
        _mla_q_kernel,
        grid=(m // tm,),
        in_specs=[
            pl.BlockSpec((tm, d), lambda i: (i, 0)),
            row(d),
            pl.BlockSpec((d, MLA_IN_PAD), lambda i: (0, 0)),
            row(MLA_Q_LORA), row(MLA_KV_LORA), row(LANES), row(LANES), row(LANES),
            pl.BlockSpec((MLA_Q_LORA, MLA_HEADS * MLA_DK_PAD), lambda i: (0, 0)),
            tab, tab, tab,
        ],
        out_specs=[
            pl.BlockSpec((tm, MLA_HEADS * MLA_DK_PAD), lambda i: (i, 0)),
            pl.BlockSpec((tm, MLA_KV_LORA), lambda i: (i, 0)),
            pl.BlockSpec((tm, LANES), lambda i: (i, 0)),
        ],
        out_shape=[
            jax.ShapeDtypeStruct((m, MLA_HEADS * MLA_DK_PAD), BF16),
            jax.ShapeDtypeStruct((m, MLA_KV_LORA), F32),
            jax.ShapeDtypeStruct((m, LANES), F32),
        ],
        compiler_params=_cparams("parallel"),
        name="mla_q",
    )(x, gmix.reshape(1, -1), w_in, q_norm.reshape(1, -1), kv_norm.reshape(1, -1), gq_nope.reshape(1, -1),
      gq_rope_pad, gk_rope_pad, w_qb_pad, *[_pos_table(tb, t, tm) for tb in tabs])


def _mla_kv_kernel(lat_ref, kr_ref, gk_ref, wk_ref, wv_ref, k_ref, v_ref):
    latb = lat_ref[...].astype(BF16)
    krb = kr_ref[...].astype(BF16)
    v_ref[...] = jnp.dot(latb, wv_ref[...], preferred_element_type=F32).astype(BF16)
    hg = 4
    for g0 in range(0, MLA_HEADS, hg):
        acc = jnp.dot(latb, wk_ref[:, g0 * MLA_NOPE:(g0 + hg) * MLA_NOPE], preferred_element_type=F32)
        for hh in range(hg):
            a = acc[:, hh * MLA_NOPE:(hh + 1) * MLA_NOPE]
            an = a * lax.rsqrt(jnp.mean(a * a, axis=-1, keepdims=True) + EPS) * gk_ref[...]
            col = (g0 + hh) * MLA_DK_PAD
            k_ref[:, col:col + LANES] = an.astype(BF16)
            k_ref[:, col + LANES:col + MLA_DK_PAD] = krb


def mla_kv(lat_all, kr_all, gk_nope, wk, wv):
    m = lat_all.shape[0]
    tm = _pick(m, (512, 256))
    return pl.pallas_call(
        _mla_kv_kernel,
        grid=(m // tm,),
        in_specs=[
            pl.BlockSpec((tm, MLA_KV_LORA), lambda i: (i, 0)),
            pl.BlockSpec((tm, LANES), lambda i: (i, 0)),
            pl.BlockSpec((1, MLA_NOPE), lambda i: (0, 0)),
            pl.BlockSpec((MLA_KV_LORA, MLA_HEADS * MLA_NOPE), lambda i: (0, 0)),
            pl.BlockSpec((MLA_KV_LORA, MLA_HEADS * MLA_V), lambda i: (0, 0)),
        ],
        out_specs=[
            pl.BlockSpec((tm, MLA_HEADS * MLA_DK_PAD), lambda i: (i, 0)),
            pl.BlockSpec((tm, MLA_HEADS * MLA_V), lambda i: (i, 0)),
        ],
        out_shape=[
            jax.ShapeDtypeStruct((m, MLA_HEADS * MLA_DK_PAD), BF16),
            jax.ShapeDtypeStruct((m, MLA_HEADS * MLA_V), BF16),
        ],
        compiler_params=_cparams("parallel"),
        name="mla_kv",
    )(lat_all, kr_all, gk_nope.reshape(1, -1), wk, wv)


def _head_norm(x, gain):
    lane = lax.broadcasted_iota(jnp.int32, (x.shape[0], LANES), 1)
    low = lane < FOX_DH
    outs = []
    for c in range(x.shape[1] // LANES):
        xb = x[:, c * LANES:(c + 1) * LANES]
        sq = xb * xb
        s_lo = jnp.sum(jnp.where(low, sq, 0.0), axis=-1, keepdims=True)
        s_hi = jnp.sum(jnp.where(low, 0.0, sq), axis=-1, keepdims=True)
        ms = jnp.where(low, s_lo, s_hi) * (1.0 / FOX_DH)
        outs.append(xb * lax.rsqrt(ms + EPS) * gain)
    return outs


def _fox_in_kernel(x_ref, g_ref, w_ref, gq_ref, gk_ref, bf_ref,
                   q_ref, k_ref, kb_ref, v_ref, vb_ref, gate_ref, lf_ref):
    xn = _rms_bf16(x_ref[...], g_ref[...])

    def chunks(seg):
        for lo in range(0, FOX_W, PROJ_TN):
            col = seg * FOX_W + lo
            yield lo, jnp.dot(xn, w_ref[:, col:col + PROJ_TN], preferred_element_type=F32)

    for lo, acc in chunks(0):
        for c, blk in enumerate(_head_norm(acc, gq_ref[...])):
            q_ref[:, lo + c * LANES:lo + (c + 1) * LANES] = (blk * (FOX_DH ** -0.5 * LOG2E)).astype(BF16)
    for lo, acc in chunks(1):
        for c, blk in enumerate(_head_norm(acc, gk_ref[...])):
            k_ref[:, lo + c * LANES:lo + (c + 1) * LANES] = blk
            kb_ref[:, lo + c * LANES:lo + (c + 1) * LANES] = blk.astype(BF16)
    for lo, acc in chunks(2):
        v_ref[:, lo:lo + PROJ_TN] = acc
        vb_ref[:, lo:lo + PROJ_TN] = acc.astype(BF16)
    for lo, acc in chunks(3):
        gate_ref[:, lo:lo + PROJ_TN] = acc.astype(BF16)
    fl = jnp.dot(xn, w_ref[:, 4 * FOX_W:4 * FOX_W + LANES], preferred_element_type=F32)
    lf_ref[...] = jax.nn.log_sigmoid(fl + bf_ref[...])


def fox_in(x, g, w, gq2, gk2, bf_pad):
    m, d = x.shape
    tm = _pick(m, (512, 256))
    once = pl.Buffered(1)
    seg = lambda: pl.BlockSpec((tm, FOX_W), lambda i: (i, 0))
    row = pl.BlockSpec((1, LANES), lambda i: (0, 0), pipeline_mode=once)
    return pl.pallas_call(
        _fox_in_kernel,
        grid=(m // tm,),
        in_specs=[
            pl.BlockSpec((tm, d), lambda i: (i, 0)),
            pl.BlockSpec((1, d), lambda i: (0, 0), pipeline_mode=once),
            pl.BlockSpec((d, 4 * FOX_W + LANES), lambda i: (0, 0), pipeline_mode=once),
            row, row, row,
        ],
        out_specs=[seg(), seg(), seg(), seg(), seg(), seg(),
                   pl.BlockSpec((tm, LANES), lambda i: (i, 0))],
        out_shape=[
            jax.ShapeDtypeStruct((m, FOX_W), BF16),
            jax.ShapeDtypeStruct((m, FOX_W), F32),
            jax.ShapeDtypeStruct((m, FOX_W), BF16),
            jax.ShapeDtypeStruct((m, FOX_W), F32),
            jax.ShapeDtypeStruct((m, FOX_W), BF16),
            jax.ShapeDtypeStruct((m, FOX_W), BF16),
            jax.ShapeDtypeStruct((m, LANES), F32),
        ],
        compiler_params=_cparams("parallel", vmem=VMEM_LIMIT_BIG),
        name="fox_in",
    )(x, g.reshape(1, d), w, gq2, gk2, bf_pad)


def _fox_cumsum_kernel(lf_ref, ft_ref, fh_ref):
    tkp = lf_ref.shape[1]
    x = lf_ref[0].T[0:FOX_HEADS, :]
    lane = lax.broadcasted_iota(jnp.int32, x.shape, 1)
    shift = 1
    while shift < tkp:
        x = x + jnp.where(lane >= shift, pltpu.roll(x, shift, 1), 0.0)
        shift *= 2
    x = x * LOG2E
    for h in range(FOX_HEADS):
        fh_ref[0, h] = x[h:h + 1, :]
    ft_ref[0] = jnp.concatenate([x, jnp.zeros((LANES - FOX_HEADS, tkp), F32)], axis=0).T


def fox_cumsum(lf_all):
    b, tkp, _ = lf_all.shape
    return pl.pallas_call(
        _fox_cumsum_kernel,
        grid=(b,),
        in_specs=[pl.BlockSpec((1, tkp, LANES), lambda i: (i, 0, 0))],
        out_specs=[pl.BlockSpec((1, tkp, LANES), lambda i: (i, 0, 0)),
                   pl.BlockSpec((1, FOX_HEADS, 1, tkp), lambda i: (i, 0, 0, 0))],
        out_shape=[jax.ShapeDtypeStruct((b, tkp, LANES), F32),
                   jax.ShapeDtypeStruct((b, FOX_HEADS, 1, tkp), F32)],
        compiler_params=_cparams("parallel"),
        name="fox_cumsum",
    )(lf_all)


def _rope_tables(pos, half):
    inv = ROPE_THETA ** (-jnp.arange(half, dtype=F32) / half)
    ang = pos.astype(F32)[:, None] * inv[None, :]
    return jnp.cos(ang), jnp.sin(ang)


def _pad_rows(a, rows):
    pad = rows - a.shape[1]
    if pad == 0:
        return a
    return jnp.pad(a, ((0, 0), (0, pad)) + ((0, 0),) * (a.ndim - 2))


def _retention_mixer(h2, b, t, pos, s0, gmix, w_in, gn, w_out):
    cos, sin = _rope_tables(pos, RET_DK // 2)
    q, k, v, gate = ret_in(h2, gmix, w_in, cos, sin, t)
    lg = jnp.log1p(-jnp.exp2(-5.0 - jnp.arange(RET_HEADS, dtype=F32)))
    chunk = _pick(t, (256, 128, 64))
    a, s = retention(q.reshape(b, t, RET_QK), k.reshape(b, t, RET_QK), v.reshape(b, t, RET_VW),
                     gate.reshape(b, t, RET_VW), gn.reshape(1, RET_VW), s0, lg, chunk)
    return matmul_res(a.reshape(b * t, RET_VW), w_out, h2, name="ret_out"), s


def _mla_mixer(h2, b, t, pos, lat_past, kr_past, gmix, w):
    cos, sin = _rope_tables(pos, MLA_ROPE // 2)
    z = jnp.zeros_like(cos)
    tabs = (jnp.concatenate([cos, cos, z, z], axis=1), jnp.concatenate([-sin, z, z, z], axis=1),
            jnp.concatenate([z, sin, z, z], axis=1))
    q_pad, lat, kr_pad = mla_q(h2, gmix, w['w_in'], w['q_norm'], w['kv_norm'], w['gq_nope'], w['gq_rope_pad'],
                               w['gk_rope_pad'], w['w_qb_pad'], tabs, t)
    tk = lat_past.shape[1] + t
    tkp = -(-tk // LANES) * LANES
    if lat_past.shape[1] == 0:
        lat_all, kr_all = lat, kr_pad
    else:
        lat_all = _pad_rows(jnp.concatenate([lat_past, lat.reshape(b, t, MLA_KV_LORA)], axis=1), tkp)
        kr_past_pad = jnp.pad(kr_past, ((0, 0), (0, 0), (0, LANES - MLA_ROPE)))
        kr_all = _pad_rows(jnp.concatenate([kr_past_pad, kr_pad.reshape(b, t, LANES)], axis=1), tkp)
        lat_all = lat_all.reshape(b * tkp, MLA_KV_LORA)
        kr_all = kr_all.reshape(b * tkp, LANES)
    k_pad, v = mla_kv(lat_all, kr_all, w['gk_nope'], w['w_kb'], w['w_vb'])
    o = attention(q_pad.reshape(b, t, -1), k_pad.reshape(b, tkp, -1), v.reshape(b, tkp, -1), mode="chunk",
                  n_blocks=MLA_HEADS // ATTN_NSUB, nsub=ATTN_NSUB, hpb=1, tk_real=tk, name="mla_attn")
    h2 = matmul_res(o.reshape(b * t, MLA_HEADS * MLA_V), w['w_out'], h2, name="mla_out")
    return h2, lat.reshape(b, t, MLA_KV_LORA), kr_pad[:, :MLA_ROPE].reshape(b, t, MLA_ROPE)


def _fox_mixer(h2, b, t, k_past, v_past, lf_past, gmix, w):
    q, k, kb, v, vb, gate, lf_pad = fox_in(h2, gmix, w['w_in'], w['gq2'], w['gk2'], w['bf_pad'])
    lf = lf_pad[:, :FOX_HEADS].reshape(b, t, FOX_HEADS)
    tp = k_past.shape[1]
    tk = tp + t
    tkp = -(-tk // LANES) * LANES
    kb = kb.reshape(b, t, FOX_W)
    vb = vb.reshape(b, t, FOX_W)
    if tp > 0:
        kb = _pad_rows(jnp.concatenate([k_past.reshape(b, tp, FOX_W).astype(BF16), kb], axis=1), tkp)
        vb = _pad_rows(jnp.concatenate([v_past.reshape(b, tp, FOX_W).astype(BF16), vb], axis=1), tkp)
    lf_all = lf_pad.reshape(b, t, LANES)
    if tp > 0:
        lf_all = _pad_rows(jnp.concatenate([_pad_last(lf_past.astype(F32), LANES), lf_all], axis=1), tkp)
    f_time, fk = fox_cumsum(lf_all)
    fq = f_time[:, tk - t:tk]
    a = attention(q.reshape(b, t, FOX_W), kb, vb, mode="token", n_blocks=FOX_HEADS // (2 * ATTN_NSUB),
                  nsub=ATTN_NSUB, hpb=2, tk_real=tk, name="fox_attn", fq=fq, fk=fk,
                  gate=gate.reshape(b, t, FOX_W))
    h2 = matmul_res(a.reshape(b * t, FOX_W), w['w_out'], h2, name="fox_out")
    return (h2, k.reshape(b, t, FOX_HEADS, FOX_DH), v.reshape(b, t, FOX_HEADS, FOX_DH), lf)


def _trunk(x, p, past_len, ret_states, lat_past, kr_past, fk_past, fv_past, flf_past, w):
    b, t, d = x.shape
    pos = past_len + jnp.arange(t)
    h2 = x.reshape(b * t, d)
    new_ret, new_lat, new_kr, new_fk, new_fv, new_flf = [], [], [], [], [], []
    for i in range(DEPTH):
        kind, j = i % N_MIXERS, i // N_MIXERS
        if kind == 0:
            h2, s = _retention_mixer(h2, b, t, pos, ret_states[j], w['norm_mix'][i], w['ret_w_in'][j],
                                     w['ret_gn'][j], w['ret_w_out'][j])
            new_ret.append(s)
        elif kind == 1:
            h2, lat, kr = _mla_mixer(h2, b, t, pos, lat_past[j], kr_past[j], w['norm_mix'][i], w['mla'][j])
            new_lat.append(lat)
            new_kr.append(kr)
        else:
            h2, fk, fv, flf = _fox_mixer(h2, b, t, fk_past[j], fv_past[j], flf_past[j], w['norm_mix'][i],
                                         w['fox'][j])
            new_fk.append(fk)
            new_fv.append(fv)
            new_flf.append(flf)
        h2 = ffn_pe(h2, p.reshape(DEPTH, b * t, PE_DIM), i, w['norm_ffn'][i], w['ffn_w_gate'][i],
                    w['ffn_w_up'][i], w['ffn_w_down'][i], w['norm_pe'][i], w['pe_w_gate'][i], w['pe_w_proj'][i],
                    final_gain=w['norm_final'] if i == DEPTH - 1 else None)
    return (h2.reshape(b, t, d), jnp.stack(new_ret), jnp.stack(new_lat), jnp.stack(new_kr),
            jnp.stack(new_fk), jnp.stack(new_fv), jnp.stack(new_flf))


def _pad_last(a, n):
    return jnp.pad(a, [(0, 0)] * (a.ndim - 1) + [(0, n - a.shape[-1])])


def kernel(x_prompt, x_sample, state_ret, cache_mla_latent, cache_mla_krope, cache_fox_k, cache_fox_v,
           cache_fox_logf, p_prompt, p_sample, norm_mix, norm_ffn, norm_pe, norm_final, ret_w_in, ret_gn,
           ret_w_out, mla_w_in, mla_q_norm, mla_kv_norm, mla_w_qb, mla_w_kvb, mla_gq_nope, mla_gq_rope,
           mla_gk_nope, mla_gk_rope, mla_w_out, fox_w_in, fox_b_f, fox_gq, fox_gk, fox_w_out, ffn_w_gate,
           ffn_w_up, ffn_w_down, pe_w_proj, pe_w_gate):
    bf = lambda a: a.astype(BF16)
    n_ret, n_mla, n_fox = state_ret.shape[0], cache_mla_latent.shape[0], cache_fox_k.shape[0]
    mla = []
    qscale = (MLA_NOPE + MLA_ROPE) ** -0.5 * LOG2E
    for j in range(n_mla):
        w_qb = mla_w_qb[j].reshape(MLA_Q_LORA, MLA_HEADS, MLA_NOPE + MLA_ROPE)
        w_kvb = mla_w_kvb[j].reshape(MLA_KV_LORA, MLA_HEADS, MLA_NOPE + MLA_V)
        mla.append(dict(
            w_in=bf(_pad_last(mla_w_in[j], MLA_IN_PAD)),
            q_norm=mla_q_norm[j], kv_norm=mla_kv_norm[j], gk_nope=mla_gk_nope[j],
            gq_nope=mla_gq_nope[j] * qscale,
            gq_rope_pad=_pad_last(mla_gq_rope[j] * qscale, LANES).reshape(1, LANES),
            gk_rope_pad=_pad_last(mla_gk_rope[j], LANES).reshape(1, LANES),
            w_qb_pad=bf(_pad_last(w_qb, MLA_DK_PAD).reshape(MLA_Q_LORA, MLA_HEADS * MLA_DK_PAD)),
            w_kb=bf(w_kvb[:, :, :MLA_NOPE].reshape(MLA_KV_LORA, MLA_HEADS * MLA_NOPE)),
            w_vb=bf(w_kvb[:, :, MLA_NOPE:].reshape(MLA_KV_LORA, MLA_HEADS * MLA_V)),
            w_out=bf(mla_w_out[j])))
    fox = []
    for j in range(n_fox):
        fox.append(dict(
            w_in=bf(_pad_last(fox_w_in[j], 4 * FOX_W + LANES)),
            gq2=jnp.tile(fox_gq[j], 2).reshape(1, LANES), gk2=jnp.tile(fox_gk[j], 2).reshape(1, LANES),
            bf_pad=_pad_last(fox_b_f[j], LANES).reshape(1, LANES),
            w_out=bf(fox_w_out[j])))
    w = dict(norm_mix=norm_mix, norm_ffn=norm_ffn, norm_pe=norm_pe, norm_final=norm_final,
             ret_w_in=bf(ret_w_in), ret_gn=ret_gn, ret_w_out=bf(ret_w_out), mla=mla, fox=fox,
             ffn_w_gate=bf(ffn_w_gate), ffn_w_up=bf(ffn_w_up), ffn_w_down=bf(ffn_w_down),
             pe_w_proj=bf(pe_w_proj), pe_w_gate=bf(pe_w_gate))
    bp = x_prompt.shape[0]
    dt = x_prompt.dtype
    y_p, ret_p, lat_p, kr_p, fk_p, fv_p, flf_p = _trunk(
        x_prompt, p_prompt, 0,
        jnp.zeros((n_ret, bp, RET_HEADS, RET_DK, RET_DV), dt),
        jnp.zeros((n_mla, bp, 0, MLA_KV_LORA), dt), jnp.zeros((n_mla, bp, 0, MLA_ROPE), dt),
        jnp.zeros((n_fox, bp, 0, FOX_HEADS, FOX_DH), dt), jnp.zeros((n_fox, bp, 0, FOX_HEADS, FOX_DH), dt),
        jnp.zeros((n_fox, bp, 0, FOX_HEADS), dt), w)
    y_s, ret_s, lat_s, kr_s, fk_s, fv_s, flf_s = _trunk(
        x_sample, p_sample, cache_mla_latent.shape[2], state_ret, cache_mla_latent, cache_mla_krope,
        cache_fox_k, cache_fox_v, cache_fox_logf, w)
    return (y_p, y_s, ret_p, ret_s, lat_p, kr_p, lat_s, kr_s, fk_p, fv_p, flf_p, fk_s, fv_s, flf_s)
```
